```python
import math
import jax, jax.numpy as jnp
from jax import lax
import numpy as np

D_MODEL = 1024
BATCH = 8
SEQ = 4096
DEPTH = 2

ATT_HEADS = 4
ATT_QK_DIM = 64
ATT_V_DIM = 2 * ATT_QK_DIM
ATT_WIDTH = ATT_HEADS * ATT_V_DIM
Q_BLOCK = 128
NEG_LOGIT = -1e30
REL_BUCKETS = 32
REL_MAX_DIST = 128
LRU_WIDTH = 512
LRU_BLOCKS = 8
LRU_BLOCK_W = LRU_WIDTH // LRU_BLOCKS
LRU_CONV = 4
LRU_C = 8.0
SC_WIDTH = 512
SC_CONV = 3
CF_WIDTH = 512
CF_CONV = 31
N_BRANCH = 4
BRANCH_WIDTH = 512
FFN_HIDDEN = -(-8 * D_MODEL // (3 * 256)) * 256
PLE_DIM = 256
EPS = 1e-6

IN_WIDTHS = (ATT_WIDTH, ATT_WIDTH, ATT_WIDTH,
             LRU_WIDTH, LRU_WIDTH,
             SC_WIDTH, SC_WIDTH, SC_WIDTH,
             CF_WIDTH, CF_WIDTH,
             N_BRANCH * D_MODEL)
IN_TOTAL = sum(IN_WIDTHS)

kernel_name = "hybrid_parallel_gated_mixers"


def rmsnorm(x, g):
    xf = x.astype(jnp.float32)
    y = xf * lax.rsqrt(jnp.mean(xf * xf, axis=-1, keepdims=True) + EPS)
    return (y * g.astype(jnp.float32)).astype(x.dtype)


def layernorm(x, g, b):
    xf = x.astype(jnp.float32)
    mu = jnp.mean(xf, axis=-1, keepdims=True)
    var = jnp.mean(jnp.square(xf - mu), axis=-1, keepdims=True)
    y = (xf - mu) * lax.rsqrt(var + EPS)
    return (y * g.astype(jnp.float32) + b.astype(jnp.float32)).astype(x.dtype)


def causal_dwconv(x, w):
    k = w.shape[0]
    return lax.conv_general_dilated(
        x, w[:, None, :].astype(x.dtype), window_strides=(1,), padding=[(k - 1, 0)],
        dimension_numbers=("NWC", "WIO", "NWC"), feature_group_count=x.shape[-1])


def t5_bucket(rel):
    n = jnp.maximum(rel, 0)
    max_exact = REL_BUCKETS // 2
    nf = jnp.maximum(n, 1).astype(jnp.float32)
    large = max_exact + (jnp.log(nf / max_exact) / math.log(REL_MAX_DIST / max_exact)
                         * (REL_BUCKETS - max_exact)).astype(jnp.int32)
    large = jnp.minimum(large, REL_BUCKETS - 1)
    return jnp.where(n < max_exact, n, large)


def diff_attention(q, k, v, rel_bias, lam, sub_g, lam_init):
    b, s, _ = q.shape
    nb = s // Q_BLOCK
    q = q.reshape(b, s, ATT_HEADS, ATT_V_DIM)
    k = k.reshape(b, s, ATT_HEADS, ATT_V_DIM)
    v = v.reshape(b, s, ATT_HEADS, ATT_V_DIM)
    k1, k2 = k[..., :ATT_QK_DIM], k[..., ATT_QK_DIM:]
    scale = ATT_QK_DIM ** -0.5
    qb = q.reshape(b, nb, Q_BLOCK, ATT_HEADS, ATT_V_DIM).transpose(1, 0, 2, 3, 4)
    bias_table = rel_bias.astype(jnp.float32)
    kpos = jnp.arange(s)

    def block(args):
        qblk, bi = args
        qpos = bi * Q_BLOCK + jnp.arange(Q_BLOCK)
        rel = qpos[:, None] - kpos[None, :]
        bias = jnp.transpose(bias_table[t5_bucket(rel)], (2, 0, 1))
        causal = rel >= 0

        def softmax_map(qh, kh):
            sc = jnp.einsum("bqhd,bkhd->bhqk", qh, kh).astype(jnp.float32) * scale + bias
            sc = jnp.where(causal, sc, NEG_LOGIT)
            return jax.nn.softmax(sc, axis=-1)

        probs = softmax_map(qblk[..., :ATT_QK_DIM], k1) - lam * softmax_map(qblk[..., ATT_QK_DIM:], k2)
        return jnp.einsum("bhqk,bkhd->bqhd", probs.astype(v.dtype), v)

    o = lax.map(block, (qb, jnp.arange(nb)))
    o = o.transpose(1, 0, 2, 3, 4).reshape(b, s, ATT_HEADS, ATT_V_DIM)
    o = rmsnorm(o, sub_g) * (1.0 - lam_init)
    return o.reshape(b, s, ATT_WIDTH)


def rglru_branch(xr, gate_in, conv_w, conv_b, wa, ba, wx, bx, lam_param):
    b, s, _ = xr.shape
    xc = causal_dwconv(xr, conv_w) + conv_b.astype(xr.dtype)
    xb = xc.reshape(b, s, LRU_BLOCKS, LRU_BLOCK_W)
    r = jax.nn.sigmoid(jnp.einsum("bshi,hij->bshj", xb, wa) + ba).reshape(b, s, LRU_WIDTH)
    i = jax.nn.sigmoid(jnp.einsum("bshi,hij->bshj", xb, wx) + bx).reshape(b, s, LRU_WIDTH)
    log_a = -LRU_C * r.astype(jnp.float32) * jax.nn.softplus(-lam_param.astype(jnp.float32))
    a = jnp.exp(log_a)
    mult = jnp.sqrt(-jnp.expm1(2.0 * log_a))
    u = mult * (i * xc).astype(jnp.float32)

    def combine(left, right):
        a1, b1 = left
        a2, b2 = right
        return a1 * a2, a2 * b1 + b2

    _, h = lax.associative_scan(combine, (a, u), axis=1)
    return h.astype(xr.dtype) * jax.nn.gelu(gate_in)


def short_conv_branch(bg, cg, xs, conv_w):
    return bg * causal_dwconv(cg * xs, conv_w)


def conformer_branch(va, vg, conv_w, conv_b, ln_g, ln_b):
    u = va * jax.nn.sigmoid(vg)
    u = causal_dwconv(u, conv_w) + conv_b.astype(u.dtype)
    return jax.nn.silu(layernorm(u, ln_g, ln_b))


def setup_inputs(seed: int = 0) -> dict:
    key = jax.random.key(seed)
    ks = iter(jax.random.split(key, 40))
    f32 = jnp.float32

    def nrm(shape, scale):
        return jax.random.normal(next(ks), shape, f32) * scale

    def gain(shape):
        return 1.0 + nrm(shape, 0.02)

    u = jax.random.uniform(next(ks), (DEPTH, LRU_WIDTH), f32, 0.9, 0.999)
    a0 = u ** (1.0 / LRU_C)
    lru_lambda = jnp.log(a0) - jnp.log1p(-a0)
    return {
        "x": nrm((BATCH, SEQ, D_MODEL), 1.0),
        "p": nrm((DEPTH, BATCH, SEQ, PLE_DIM), 1.0),
        "rel_bias": nrm((REL_BUCKETS, ATT_HEADS), 0.5),
        "g_pre_mix": gain((DEPTH, D_MODEL)),
        "w_in": nrm((DEPTH, D_MODEL, IN_TOTAL), D_MODEL ** -0.5),
        "att_lambda": nrm((DEPTH, 4, ATT_QK_DIM), 0.1),
        "att_subnorm_g": gain((DEPTH, ATT_V_DIM)),
        "lru_conv_w": nrm((DEPTH, LRU_CONV, LRU_WIDTH), LRU_CONV ** -0.5),
        "lru_conv_b": nrm((DEPTH, LRU_WIDTH), 0.02),
        "lru_wa": nrm((DEPTH, LRU_BLOCKS, LRU_BLOCK_W, LRU_BLOCK_W), LRU_BLOCK_W ** -0.5),
        "lru_ba": nrm((DEPTH, LRU_BLOCKS, LRU_BLOCK_W), 0.02),
        "lru_wx": nrm((DEPTH, LRU_BLOCKS, LRU_BLOCK_W, LRU_BLOCK_W), LRU_BLOCK_W ** -0.5),
        "lru_bx": nrm((DEPTH, LRU_BLOCKS, LRU_BLOCK_W), 0.02),
        "lru_lambda": lru_lambda,
        "sc_conv_w": nrm((DEPTH, SC_CONV, SC_WIDTH), SC_CONV ** -0.5),
        "cf_conv_w": nrm((DEPTH, CF_CONV, CF_WIDTH), CF_CONV ** -0.5),
        "cf_conv_b": nrm((DEPTH, CF_WIDTH), 0.02),
        "cf_ln_g": gain((DEPTH, CF_WIDTH)),
        "cf_ln_b": nrm((DEPTH, CF_WIDTH), 0.02),
        "gate_b": nrm((DEPTH, N_BRANCH, D_MODEL), 0.02),
        "w_branch": nrm((DEPTH, N_BRANCH, BRANCH_WIDTH, D_MODEL), BRANCH_WIDTH ** -0.5),
        "w_o": nrm((DEPTH, D_MODEL, D_MODEL), D_MODEL ** -0.5),
        "g_post_mix": gain((DEPTH, D_MODEL)),
        "g_pre_ffn": gain((DEPTH, D_MODEL)),
        "w_ffn_in": nrm((DEPTH, D_MODEL, 2 * FFN_HIDDEN), D_MODEL ** -0.5),
        "w_ffn_out": nrm((DEPTH, FFN_HIDDEN, D_MODEL), FFN_HIDDEN ** -0.5),
        "g_post_ffn": gain((DEPTH, D_MODEL)),
        "w_ple_in": nrm((DEPTH, PLE_DIM, D_MODEL), PLE_DIM ** -0.5),
        "g_ple": gain((DEPTH, D_MODEL)),
        "w_ple_gate": nrm((DEPTH, D_MODEL, D_MODEL), D_MODEL ** -0.5),
    }


def reference(x, p, rel_bias, g_pre_mix, w_in, att_lambda, att_subnorm_g, lru_conv_w, lru_conv_b,
              lru_wa, lru_ba, lru_wx, lru_bx, lru_lambda, sc_conv_w, cf_conv_w, cf_conv_b, cf_ln_g,
              cf_ln_b, gate_b, w_branch, w_o, g_post_mix, g_pre_ffn, w_ffn_in, w_ffn_out, g_post_ffn,
              w_ple_in, g_ple, w_ple_gate):
    b, s, d = x.shape
    split_points = []
    acc = 0
    for wdt in IN_WIDTHS[:-1]:
        acc += wdt
        split_points.append(acc)

    for l in range(DEPTH):
        h = rmsnorm(x, g_pre_mix[l])
        z = h @ w_in[l]
        (q, k, v, lru_x, lru_g, sc_b, sc_c, sc_x, cf_a, cf_g, gate_logits) = jnp.split(z, split_points, axis=-1)

        lam_init = 0.8 - 0.6 * math.exp(-0.3 * l)
        lv = att_lambda[l].astype(jnp.float32)
        lam = jnp.exp(jnp.sum(lv[0] * lv[1])) - jnp.exp(jnp.sum(lv[2] * lv[3])) + lam_init

        y_att = diff_attention(q, k, v, rel_bias, lam, att_subnorm_g[l], lam_init)
        y_lru = rglru_branch(lru_x, lru_g, lru_conv_w[l], lru_conv_b[l], lru_wa[l], lru_ba[l],
                             lru_wx[l], lru_bx[l], lru_lambda[l])
        y_sc = short_conv_branch(sc_b, sc_c, sc_x, sc_conv_w[l])
        y_cf = conformer_branch(cf_a, cf_g, cf_conv_w[l], cf_conv_b[l], cf_ln_g[l], cf_ln_b[l])

        gates = jax.nn.sigmoid(gate_logits.reshape(b, s, N_BRANCH, d) + gate_b[l])
        merged = gates[:, :, 0] * (y_att @ w_branch[l, 0])
        merged = merged + gates[:, :, 1] * (y_lru @ w_branch[l, 1])
        merged = merged + gates[:, :, 2] * (y_sc @ w_branch[l, 2])
        merged = merged + gates[:, :, 3] * (y_cf @ w_branch[l, 3])
        x = x + rmsnorm(merged @ w_o[l], g_post_mix[l])

        h2 = rmsnorm(x, g_pre_ffn[l])
        gt, up = jnp.split(h2 @ w_ffn_in[l], 2, axis=-1)
        x = x + rmsnorm((jax.nn.silu(gt) * up) @ w_ffn_out[l], g_post_ffn[l])

        e = p[l] @ w_ple_in[l]
        ge = jax.nn.sigmoid(rmsnorm(x, g_ple[l]) @ w_ple_gate[l])
        x = x + ge * e
    return x
```

```python
import functools
import math

import numpy as np
import jax
import jax.numpy as jnp
from jax import lax
from jax.experimental import pallas as pl
from jax.experimental.pallas import tpu as pltpu

D_MODEL = 1024
ATT_HEADS = 4
ATT_QK_DIM = 64
ATT_V_DIM = 128
ATT_WIDTH = 512
REL_BUCKETS = 32
REL_MAX_DIST = 128
NEG_LOGIT = -1e30
LRU_WIDTH = 512
LRU_BLOCKS = 8
LRU_CONV = 4
LRU_C = 8.0
SC_CONV = 3
CF_CONV = 31
BRANCH_WIDTH = 512
FFN_HIDDEN = 2816
PLE_DIM = 256
EPS = 1e-6
QKV_WIDTH = 3 * ATT_WIDTH
REST_WIDTH = 7 * BRANCH_WIDTH + 4 * D_MODEL

TM_PROJ = 512
TQ = 256
TK = 256
TM_MIX = 256
TM_FFN = 256
LRU_HALO = 8
SC_HALO = 8
CF_HALO = 32
VMEM_LIMIT = 56 * 1024 * 1024


def _t5_large_thresholds():
    max_exact = REL_BUCKETS // 2
    n = np.arange(1, 4 * REL_MAX_DIST, dtype=np.float64)
    large = max_exact + (np.log(n / max_exact) / math.log(REL_MAX_DIST / max_exact)
                         * (REL_BUCKETS - max_exact)).astype(np.int64)
    bucket = np.where(n < max_exact, n.astype(np.int64), np.minimum(large, REL_BUCKETS - 1))
    return [int(np.argmax(bucket >= b)) + 1 for b in range(max_exact + 1, REL_BUCKETS)]


T5_THRESHOLDS = _t5_large_thresholds()
assert T5_THRESHOLDS[-1] <= TK + 1


def _const_spec(block_shape, index_map):
    return pl.BlockSpec(block_shape, index_map, pipeline_mode=pl.Buffered(1))


def _rmsnorm(x, g):
    return x * lax.rsqrt(jnp.mean(x * x, axis=-1, keepdims=True) + EPS) * g


def _bf16(x):
    return x.astype(jnp.bfloat16)


def _dot(a, b):
    return jnp.dot(a, b, preferred_element_type=jnp.float32)


def _bias_kernel(table_ref, o_ref):
    h = pl.program_id(0)
    d = pl.program_id(1)
    i = lax.broadcasted_iota(jnp.int32, (TQ, TK), 0)
    j = lax.broadcasted_iota(jnp.int32, (TQ, TK), 1)
    n = jnp.maximum(d * TK + i - j, 0)
    max_exact = REL_BUCKETS // 2
    large = jnp.full((TQ, TK), max_exact, jnp.int32)
    for thr in T5_THRESHOLDS:
        large = large + (n >= thr).astype(jnp.int32)
    bucket = jnp.where(n < max_exact, n, large)
    val = jnp.zeros((TQ, TK), jnp.float32)
    for b in range(REL_BUCKETS):
        val = jnp.where(bucket == b, table_ref[b, h], val)
    o_ref[0, 0] = val


def _bias_blocks(rel_bias):
    return pl.pallas_call(
        _bias_kernel,
        grid=(ATT_HEADS, 2),
        in_specs=[pl.BlockSpec(memory_space=pltpu.SMEM)],
        out_specs=pl.BlockSpec((1, 1, TQ, TK), lambda h, d: (h, d, 0, 0)),
        out_shape=jax.ShapeDtypeStruct((ATT_HEADS, 2, TQ, TK), jnp.float32),
        name="t5_bias_blocks",
    )(rel_bias)


def _qkv_kernel(x_ref, g_ref, w_ref, o_ref):
    hb = _bf16(_rmsnorm(x_ref[...], g_ref[0]))
    z = _dot(hb, w_ref[0])
    o_ref[:, :ATT_WIDTH] = _bf16(z[:, :ATT_WIDTH] * (ATT_QK_DIM ** -0.5))
    o_ref[:, ATT_WIDTH:] = _bf16(z[:, ATT_WIDTH:])


def _qkv_proj(x2d, g, w_qkv, layer):
    t = x2d.shape[0]
    return pl.pallas_call(
        _qkv_kernel,
        grid=(t // TM_PROJ,),
        in_specs=[
            pl.BlockSpec((TM_PROJ, D_MODEL), lambda i: (i, 0)),
            _const_spec((1, 1, D_MODEL), lambda i: (layer, 0, 0)),
            _const_spec((1, D_MODEL, QKV_WIDTH), lambda i: (layer, 0, 0)),
        ],
        out_specs=pl.BlockSpec((TM_PROJ, QKV_WIDTH), lambda i: (i, 0)),
        out_shape=jax.ShapeDtypeStruct((t, QKV_WIDTH), jnp.bfloat16),
        compiler_params=pltpu.CompilerParams(
            dimension_semantics=("arbitrary",), vmem_limit_bytes=VMEM_LIMIT),
        name="qkv_proj",
    )(x2d, g, w_qkv)


def _attn_kernel(lam_init, table_ref, q_ref, k_ref, v_ref, bias_ref, lam_ref, subg_ref, o_ref,
                 m_sc, l_sc, acc_sc):
    h = pl.program_id(1)
    i = pl.program_id(2)
    q = q_ref[0]
    lane = lax.broadcasted_iota(jnp.int32, q.shape, 1)
    zero = jnp.zeros_like(q)
    qs = jnp.concatenate([jnp.where(lane < ATT_QK_DIM, q, zero),
                          jnp.where(lane >= ATT_QK_DIM, q, zero)], axis=0)

    m_sc[...] = jnp.full(m_sc.shape, NEG_LOGIT, jnp.float32)
    l_sc[...] = jnp.zeros(l_sc.shape, jnp.float32)
    acc_sc[...] = jnp.zeros(acc_sc.shape, jnp.float32)

    def block(j, add_bias):
        start = pl.multiple_of(j * TK, TK)
        k = k_ref[0, pl.ds(start, TK), :]
        v = v_ref[0, pl.ds(start, TK), :]
        s = lax.dot_general(qs, k, (((1,), (1,)), ((), ())), preferred_element_type=jnp.float32)
        s = add_bias(s)
        m_prev = m_sc[...]
        m_new = jnp.maximum(m_prev, jnp.max(s, axis=1, keepdims=True))
        alpha = jnp.exp(m_prev - m_new)
        p = jnp.exp(s - m_new)
        l_sc[...] = alpha * l_sc[...] + jnp.sum(p, axis=1, keepdims=True)
        acc_sc[...] = alpha * acc_sc[...] + _dot(_bf16(p), v)
        m_sc[...] = m_new

    far_bias = table_ref[REL_BUCKETS - 1, h]

    def far_body(j, carry):
        block(j, lambda s: s + far_bias)
        return carry

    lax.fori_loop(0, jnp.maximum(i - 1, 0), far_body, 0)

    def near_bias(s):
        return (s.reshape(2, TQ, TK) + bias_ref[0, 1][None]).reshape(2 * TQ, TK)

    def diag_bias(s):
        row = lax.broadcasted_iota(jnp.int32, (TQ, TK), 0)
        col = lax.broadcasted_iota(jnp.int32, (TQ, TK), 1)
        s3 = s.reshape(2, TQ, TK) + bias_ref[0, 0][None]
        s3 = jnp.where((row >= col)[None], s3, NEG_LOGIT)
        return s3.reshape(2 * TQ, TK)

    @pl.when(i >= 1)
    def _():
        block(i - 1, near_bias)

    block(i, diag_bias)

    lv = lam_ref[0]
    lam = (jnp.exp(jnp.sum(lv[0:1] * lv[1:2], axis=1, keepdims=True))
           - jnp.exp(jnp.sum(lv[2:3] * lv[3:4], axis=1, keepdims=True)) + lam_init)
    o = acc_sc[...] / l_sc[...]
    o = o[:TQ] - lam * o[TQ:]
    o = _rmsnorm(o, subg_ref[0]) * (1.0 - lam_init)
    o_ref[0] = _bf16(o)


def _diff_attention(qkv, rel_bias, bias_blocks, att_lambda, att_subnorm_g, layer, lam_init):
    b, s, _ = qkv.shape
    nh = ATT_HEADS
    return pl.pallas_call(
        functools.partial(_attn_kernel, lam_init),
        grid=(b, nh, s // TQ),
        in_specs=[
            pl.BlockSpec(memory_space=pltpu.SMEM),
            pl.BlockSpec((1, TQ, ATT_V_DIM), lambda bi, h, i: (bi, i, h)),
            pl.BlockSpec((1, s, ATT_V_DIM), lambda bi, h, i: (bi, 0, nh + h)),
            pl.BlockSpec((1, s, ATT_V_DIM), lambda bi, h, i: (bi, 0, 2 * nh + h)),
            pl.BlockSpec((1, 2, TQ, TK), lambda bi, h, i: (h, 0, 0, 0)),
            pl.BlockSpec((1, 4, ATT_QK_DIM), lambda bi, h, i: (layer, 0, 0)),
            pl.BlockSpec((1, 1, ATT_V_DIM), lambda bi, h, i: (layer, 0, 0)),
        ],
        out_specs=pl.BlockSpec((1, TQ, ATT_V_DIM), lambda bi, h, i: (bi, i, h)),
        out_shape=jax.ShapeDtypeStruct((b, s, ATT_WIDTH), jnp.bfloat16),
        scratch_shapes=[
            pltpu.VMEM((2 * TQ, 1), jnp.float32),
            pltpu.VMEM((2 * TQ, 1), jnp.float32),
            pltpu.VMEM((2 * TQ, ATT_V_DIM), jnp.float32),
        ],
        compiler_params=pltpu.CompilerParams(
            dimension_semantics=("arbitrary", "arbitrary", "arbitrary"), vmem_limit_bytes=VMEM_LIMIT),
        name="diff_attention",
    )(rel_bias, qkv, qkv, qkv, bias_blocks, att_lambda, att_subnorm_g)


def _causal_conv(buf_ref, halo, cur, w_ref, ktaps):
    tm = cur.shape[0]
    buf_ref[halo:halo + tm, :] = cur
    acc = None
    for k in range(ktaps):
        off = halo - (ktaps - 1) + k
        term = w_ref[0, k:k + 1, :] * buf_ref[off:off + tm, :]
        acc = term if acc is None else acc + term
    buf_ref[0:halo, :] = buf_ref[tm:tm + halo, :]
    return acc


def _linear_scan(a, u, carry):
    tm = a.shape[0]
    row = lax.broadcasted_iota(jnp.int32, a.shape, 0)
    d = 1
    while d < tm:
        a_s = pltpu.roll(a, d, 0)
        u_s = pltpu.roll(u, d, 0)
        keep = row >= d
        u = jnp.where(keep, a * u_s + u, u)
        a = jnp.where(keep, a * a_s, a)
        d *= 2
    return a * carry + u


def _mix_kernel(x_ref, yatt_ref, gpre_ref, w_ref, wbd_ref, bbd_ref, lcw_ref, lcb_ref, llam_ref,
                scw_ref, cfw_ref, cfb_ref, cflg_ref, cflb_ref, gb_ref, wbr_ref, wo_ref, gpost_ref,
                o_ref, lbuf, sbuf, cbuf, hcar):
    tm = TM_MIX
    bw = BRANCH_WIDTH

    @pl.when(pl.program_id(1) == 0)
    def _():
        lbuf[0:LRU_HALO, :] = jnp.zeros((LRU_HALO, bw), jnp.float32)
        sbuf[0:SC_HALO, :] = jnp.zeros((SC_HALO, bw), jnp.float32)
        cbuf[0:CF_HALO, :] = jnp.zeros((CF_HALO, bw), jnp.float32)
        hcar[...] = jnp.zeros(hcar.shape, jnp.float32)

    x = x_ref[0]
    hb = _bf16(_rmsnorm(x, gpre_ref[0]))

    def proj(idx, width=bw):
        c0 = idx * bw
        return _dot(hb, w_ref[0, :, c0:c0 + width])

    def gate(jb):
        c0 = 7 * bw + jb * D_MODEL
        return jax.nn.sigmoid(_dot(hb, w_ref[0, :, c0:c0 + D_MODEL]) + gb_ref[0, jb:jb + 1, :])

    def branch(jb, y):
        return gate(jb) * _dot(_bf16(y), wbr_ref[0, jb])

    merged = branch(0, yatt_ref[0])

    xc = _causal_conv(lbuf, LRU_HALO, proj(0), lcw_ref, LRU_CONV) + lcb_ref[0]
    ri = jax.nn.sigmoid(_dot(_bf16(xc), wbd_ref[0]) + bbd_ref[0])
    r = ri[:, :bw]
    ig = ri[:, bw:]
    log_a = -LRU_C * r * jax.nn.softplus(-llam_ref[0])
    a = jnp.exp(log_a)
    mult = jnp.sqrt(-jnp.tanh(log_a) * (a * a + 1.0))
    hs = _linear_scan(a, mult * (ig * xc), hcar[0:1, :])
    hcar[0:1, :] = hs[tm - 1:tm, :]
    merged = merged + branch(1, hs * jax.nn.gelu(proj(1)))

    sc_b = proj(2)
    cx = proj(3) * proj(4)
    merged = merged + branch(2, sc_b * _causal_conv(sbuf, SC_HALO, cx, scw_ref, SC_CONV))

    u = proj(5) * jax.nn.sigmoid(proj(6))
    u = _causal_conv(cbuf, CF_HALO, u, cfw_ref, CF_CONV) + cfb_ref[0]
    mu = jnp.mean(u, axis=-1, keepdims=True)
    uc = u - mu
    var = jnp.mean(uc * uc, axis=-1, keepdims=True)
    ln = uc * lax.rsqrt(var + EPS) * cflg_ref[0] + cflb_ref[0]
    merged = merged + branch(3, ln * jax.nn.sigmoid(ln))

    o = _dot(_bf16(merged), wo_ref[0])
    o_ref[0] = x + _rmsnorm(o, gpost_ref[0])


def _mixers(x, yatt, layer, gpre, w_rest, wbd, bbd, lcw, lcb, llam, scw, cfw, cfb, cflg, cflb,
            gate_b, w_branch, w_o, gpost):
    b, s, d = x.shape
    bw = BRANCH_WIDTH

    def lspec(*shape):
        nd = len(shape)
        return _const_spec((1,) + shape, lambda bi, i: (layer,) + (0,) * nd)

    return pl.pallas_call(
        _mix_kernel,
        grid=(b, s // TM_MIX),
        in_specs=[
            pl.BlockSpec((1, TM_MIX, d), lambda bi, i: (bi, i, 0)),
            pl.BlockSpec((1, TM_MIX, ATT_WIDTH), lambda bi, i: (bi, i, 0)),
            lspec(1, d),
            lspec(d, REST_WIDTH),
            lspec(bw, 2 * bw),
            lspec(1, 2 * bw),
            lspec(LRU_CONV, bw),
            lspec(1, bw),
            lspec(1, bw),
            lspec(SC_CONV, bw),
            lspec(CF_CONV, bw),
            lspec(1, bw),
            lspec(1, bw),
            lspec(1, bw),
            lspec(4, d),
            lspec(4, bw, d),
            lspec(d, d),
            lspec(1, d),
        ],
        out_specs=pl.BlockSpec((1, TM_MIX, d), lambda bi, i: (bi, i, 0)),
        out_shape=jax.ShapeDtypeStruct((b, s, d), jnp.float32),
        scratch_shapes=[
            pltpu.VMEM((LRU_HALO + TM_MIX, bw), jnp.float32),
            pltpu.VMEM((SC_HALO + TM_MIX, bw), jnp.float32),
            pltpu.VMEM((CF_HALO + TM_MIX, bw), jnp.float32),
            pltpu.VMEM((8, bw), jnp.float32),
        ],
        compiler_params=pltpu.CompilerParams(
            dimension_semantics=("arbitrary", "arbitrary"), vmem_limit_bytes=VMEM_LIMIT),
        name="mixers_merge",
    )(x, yatt, gpre, w_rest, wbd, bbd, lcw, lcb, llam, scw, cfw, cfb, cflg, cflb, gate_b,
      w_branch, w_o, gpost)


def _ffn_kernel(x_ref, p_ref, gpre_ref, wg_ref, wu_ref, wout_ref, gpost_ref, wple_ref, gple_ref,
                wgate_ref, o_ref):
    x = x_ref[...]
    hb = _bf16(_rmsnorm(x, gpre_ref[0]))
    gt = _dot(hb, wg_ref[0])
    up = _dot(hb, wu_ref[0])
    act = _bf16(gt * jax.nn.sigmoid(gt) * up)
    x = x + _rmsnorm(_dot(act, wout_ref[0]), gpost_ref[0])
    e = _dot(_bf16(p_ref[0]), wple_ref[0])
    ge = jax.nn.sigmoid(_dot(_bf16(_rmsnorm(x, gple_ref[0])), wgate_ref[0]))
    o_ref[...] = x + ge * e


def _ffn_ple(x2d, p3d, layer, gpre, w_gate_in, w_up_in, w_out, gpost, w_ple, g_ple, w_ple_gate):
    t, d = x2d.shape

    def lspec(*shape):
        nd = len(shape)
        return _const_spec((1,) + shape, lambda i: (layer,) + (0,) * nd)

    return pl.pallas_call(
        _ffn_kernel,
        grid=(t // TM_FFN,),
        in_specs=[
            pl.BlockSpec((TM_FFN, d), lambda i: (i, 0)),
            pl.BlockSpec((1, TM_FFN, PLE_DIM), lambda i: (layer, i, 0)),
            lspec(1, d),
            lspec(d, FFN_HIDDEN),
            lspec(d, FFN_HIDDEN),
            lspec(FFN_HIDDEN, d),
            lspec(1, d),
            lspec(PLE_DIM, d),
            lspec(1, d),
            lspec(d, d),
        ],
        out_specs=pl.BlockSpec((TM_FFN, d), lambda i: (i, 0)),
        out_shape=jax.ShapeDtypeStruct((t, d), jnp.float32),
        compiler_params=pltpu.CompilerParams(
            dimension_semantics=("arbitrary",), vmem_limit_bytes=VMEM_LIMIT),
        name="ffn_ple",
    )(x2d, p3d, gpre, w_gate_in, w_up_in, w_out, gpost, w_ple, g_ple, w_ple_gate)


def _block_diag(w):
    l, h, i, j = w.shape
    eye = jnp.eye(h, dtype=w.dtype)
    return (w[:, :, :, None, :] * eye[None, :, None, :, None]).reshape(l, h * i, h * j)


def kernel(x, p, rel_bias, g_pre_mix, w_in, att_lambda, att_subnorm_g, lru_conv_w, lru_conv_b, lru_wa, lru_ba, lru_wx, lru_bx, lru_lambda, sc_conv_w, cf_conv_w, cf_conv_b, cf_ln_g, cf_ln_b, gate_b, w_branch, w_o, g_post_mix, g_pre_ffn, w_ffn_in, w_ffn_out, g_post_ffn, w_ple_in, g_ple, w_ple_gate):
    b, s, d = x.shape
    depth = w_in.shape[0]
    t = b * s
    bf = jnp.bfloat16

    def row(v):
        return v[:, None, :]

    w_qkv = w_in[:, :, :QKV_WIDTH].astype(bf)
    w_rest = w_in[:, :, QKV_WIDTH:].astype(bf)
    wbd = jnp.concatenate([_block_diag(lru_wa), _block_diag(lru_wx)], axis=-1).astype(bf)
    bbd = jnp.concatenate([lru_ba.reshape(depth, 1, LRU_WIDTH), lru_bx.reshape(depth, 1, LRU_WIDTH)], axis=-1)
    w_branch_b = w_branch.astype(bf)
    w_o_b = w_o.astype(bf)
    w_gate_in = w_ffn_in[:, :, :FFN_HIDDEN].astype(bf)
    w_up_in = w_ffn_in[:, :, FFN_HIDDEN:].astype(bf)
    w_out_b = w_ffn_out.astype(bf)
    w_ple_b = w_ple_in.astype(bf)
    w_ple_gate_b = w_ple_gate.astype(bf)
    p3d = p.reshape(depth, t, PLE_DIM)

    bias_blocks = _bias_blocks(rel_bias)

    for layer in range(depth):
        lam_init = 0.8 - 0.6 * math.exp(-0.3 * layer)
        qkv = _qkv_proj(x.reshape(t, d), row(g_pre_mix), w_qkv, layer)
        yatt = _diff_attention(qkv.reshape(b, s, QKV_WIDTH), rel_bias, bias_blocks, att_lambda,
                               row(att_subnorm_g), layer, lam_init)
        x = _mixers(x, yatt, layer, row(g_pre_mix), w_rest, wbd, bbd, lru_conv_w, row(lru_conv_b),
                    row(lru_lambda), sc_conv_w, cf_conv_w, row(cf_conv_b), row(cf_ln_g), row(cf_ln_b),
                    gate_b, w_branch_b, w_o_b, row(g_post_mix))
        x = _ffn_ple(x.reshape(t, d), p3d, layer, row(g_pre_ffn), w_gate_in, w_up_in, w_out_b,
                     row(g_post_ffn), w_ple_b, row(g_ple), w_ple_gate_b).reshape(b, s, d)
    return x
```

```python
import functools
import math

import numpy as np
import jax
import jax.numpy as jnp
from jax import lax
from jax.experimental import pallas as pl
from jax.experimental.pallas import tpu as pltpu

D_MODEL = 1024
ATT_HEADS = 4
ATT_QK_DIM = 64
ATT_V_DIM = 128
ATT_WIDTH = 512
REL_BUCKETS = 32
REL_MAX_DIST = 128
NEG_LOGIT = -1e30
LRU_WIDTH = 512
LRU_BLOCKS = 8
LRU_CONV = 4
LRU_C = 8.0
SC_CONV = 3
CF_CONV = 31
BRANCH_WIDTH = 512
FFN_HIDDEN = 2816
PLE_DIM = 256
EPS = 1e-6
QKV_WIDTH = 3 * ATT_WIDTH
REST_WIDTH = 7 * BRANCH_WIDTH + 4 * D_MODEL

TM_PROJ = 512
TQ = 256
TK = 256
TM_MIX = 256
TM_FFN = 256
LRU_HALO = 8
SC_HALO = 8
CF_HALO = 32
VMEM_LIMIT = 56 * 1024 * 1024


def _t5_large_thresholds():
    max_exact = REL_BUCKETS // 2
    n = np.arange(1, 4 * REL_MAX_DIST, dtype=np.float64)
    large = max_exact + (np.log(n / max_exact) / math.log(REL_MAX_DIST / max_exact)
                         * (REL_BUCKETS - max_exact)).astype(np.int64)
    bucket = np.where(n < max_exact, n.astype(np.int64), np.minimum(large, REL_BUCKETS - 1))
    return [int(np.argmax(bucket >= b)) + 1 for b in range(max_exact + 1, REL_BUCKETS)]


T5_THRESHOLDS = _t5_large_thresholds()
assert T5_THRESHOLDS[-1] <= TK + 1


def _const_spec(block_shape, index_map):
    return pl.BlockSpec(block_shape, index_map, pipeline_mode=pl.Buffered(1))


def _rmsnorm(x, g):
    return x * lax.rsqrt(jnp.mean(x * x, axis=-1, keepdims=True) + EPS) * g


def _bf16(x):
    return x.astype(jnp.bfloat16)


def _dot(a, b):
    return jnp.dot(a, b, preferred_element_type=jnp.float32)


def _bias_kernel(table_ref, o_ref):
    h = pl.program_id(0)
    d = pl.program_id(1)
    i = lax.broadcasted_iota(jnp.int32, (TQ, TK), 0)
    j = lax.broadcasted_iota(jnp.int32, (TQ, TK), 1)
    n = jnp.maximum(d * TK + i - j, 0)
    max_exact = REL_BUCKETS // 2
    large = jnp.full((TQ, TK), max_exact, jnp.int32)
    for thr in T5_THRESHOLDS:
        large = large + (n >= thr).astype(jnp.int32)
    bucket = jnp.where(n < max_exact, n, large)
    val = jnp.zeros((TQ, TK), jnp.float32)
    for b in range(REL_BUCKETS):
        val = jnp.where(bucket == b, table_ref[b, h], val)
    o_ref[0, 0] = val


def _bias_blocks(rel_bias):
    return pl.pallas_call(
        _bias_kernel,
        grid=(ATT_HEADS, 2),
        in_specs=[pl.BlockSpec(memory_space=pltpu.SMEM)],
        out_specs=pl.BlockSpec((1, 1, TQ, TK), lambda h, d: (h, d, 0, 0)),
        out_shape=jax.ShapeDtypeStruct((ATT_HEADS, 2, TQ, TK), jnp.float32),
        name="t5_bias_blocks",
    )(rel_bias)


def _qkv_kernel(x_ref, g_ref, w_ref, o_ref):
    hb = _bf16(_rmsnorm(x_ref[...], g_ref[0]))
    z = _dot(hb, w_ref[0])
    o_ref[:, :ATT_WIDTH] = _bf16(z[:, :ATT_WIDTH] * (ATT_QK_DIM ** -0.5))
    o_ref[:, ATT_WIDTH:] = _bf16(z[:, ATT_WIDTH:])


def _qkv_proj(x2d, g, w_qkv, layer):
    t = x2d.shape[0]
    return pl.pallas_call(
        _qkv_kernel,
        grid=(t // TM_PROJ,),
        in_specs=[
            pl.BlockSpec((TM_PROJ, D_MODEL), lambda i: (i, 0)),
            _const_spec((1, 1, D_MODEL), lambda i: (layer, 0, 0)),
            _const_spec((1, D_MODEL, QKV_WIDTH), lambda i: (layer, 0, 0)),
        ],
        out_specs=pl.BlockSpec((TM_PROJ, QKV_WIDTH), lambda i: (i, 0)),
        out_shape=jax.ShapeDtypeStruct((t, QKV_WIDTH), jnp.bfloat16),
        compiler_params=pltpu.CompilerParams(
            dimension_semantics=("arbitrary",), vmem_limit_bytes=VMEM_LIMIT),
        name="qkv_proj",
    )(x2d, g, w_qkv)


def _attn_kernel(lam_init, table_ref, q_ref, k_ref, v_ref, bias_ref, lam_ref, subg_ref, o_ref,
                 m_sc, acc_sc):
    i = pl.program_id(1)
    rows = 2 * TQ
    dv = ATT_V_DIM
    lane = lax.broadcasted_iota(jnp.int32, (TQ, dv), 1)
    ones = jnp.ones((TK, dv), jnp.bfloat16)

    qs = []
    for h in range(ATT_HEADS):
        q = q_ref[0, :, h * dv:(h + 1) * dv]
        zero = jnp.zeros_like(q)
        qs.append(jnp.concatenate([jnp.where(lane < ATT_QK_DIM, q, zero),
                                   jnp.where(lane >= ATT_QK_DIM, q, zero)], axis=0))

    m_sc[...] = jnp.full(m_sc.shape, NEG_LOGIT, jnp.float32)
    acc_sc[...] = jnp.zeros(acc_sc.shape, jnp.float32)

    def block(h, j, add_bias):
        start = pl.multiple_of(j * TK, TK)
        k = k_ref[0, pl.ds(start, TK), h * dv:(h + 1) * dv]
        v = v_ref[0, pl.ds(start, TK), h * dv:(h + 1) * dv]
        s = lax.dot_general(qs[h], k, (((1,), (1,)), ((), ())), preferred_element_type=jnp.float32)
        s = add_bias(s)
        m_prev = m_sc[h]
        m_new = jnp.maximum(m_prev, jnp.max(s, axis=1, keepdims=True))
        alpha = jnp.exp(m_prev - m_new)
        p = jnp.exp(s - pltpu.repeat(m_new, TK // dv, axis=1))
        pv = _dot(_bf16(p), jnp.concatenate([v, ones], axis=1))
        acc_sc[h] = pltpu.repeat(alpha, 2, axis=1) * acc_sc[h] + pv
        m_sc[h] = m_new

    def far_body(j, carry):
        for h in range(ATT_HEADS):
            far_bias = table_ref[REL_BUCKETS - 1, h]
            block(h, j, lambda s: s + far_bias)
        return carry

    lax.fori_loop(0, jnp.maximum(i - 1, 0), far_body, 0)

    @pl.when(i >= 1)
    def _():
        for h in range(ATT_HEADS):
            block(h, i - 1,
                  lambda s: (s.reshape(2, TQ, TK) + bias_ref[h, 1][None]).reshape(rows, TK))

    row = lax.broadcasted_iota(jnp.int32, (TQ, TK), 0)
    col = lax.broadcasted_iota(jnp.int32, (TQ, TK), 1)
    causal = (row >= col)[None]

    def diag_bias(h):
        def f(s):
            s3 = s.reshape(2, TQ, TK) + bias_ref[h, 0][None]
            return jnp.where(causal, s3, NEG_LOGIT).reshape(rows, TK)
        return f

    for h in range(ATT_HEADS):
        block(h, i, diag_bias(h))

    lv = lam_ref[0]
    lam = (jnp.exp(jnp.sum(lv[0:1] * lv[1:2], axis=1, keepdims=True))
           - jnp.exp(jnp.sum(lv[2:3] * lv[3:4], axis=1, keepdims=True)) + lam_init)
    for h in range(ATT_HEADS):
        acc = acc_sc[h]
        o = acc[:, :dv] / acc[:, dv:]
        o = o[:TQ] - lam * o[TQ:]
        o = _rmsnorm(o, subg_ref[0]) * (1.0 - lam_init)
        o_ref[0, :, h * dv:(h + 1) * dv] = _bf16(o)


def _diff_attention(qkv, rel_bias, bias_blocks, att_lambda, att_subnorm_g, layer, lam_init):
    b, s, _ = qkv.shape
    nh = ATT_HEADS
    return pl.pallas_call(
        functools.partial(_attn_kernel, lam_init),
        grid=(b, s // TQ),
        in_specs=[
            pl.BlockSpec(memory_space=pltpu.SMEM),
            pl.BlockSpec((1, TQ, ATT_WIDTH), lambda bi, i: (bi, i, 0)),
            pl.BlockSpec((1, s, ATT_WIDTH), lambda bi, i: (bi, 0, 1)),
            pl.BlockSpec((1, s, ATT_WIDTH), lambda bi, i: (bi, 0, 2)),
            _const_spec((nh, 2, TQ, TK), lambda bi, i: (0, 0, 0, 0)),
            _const_spec((1, 4, ATT_QK_DIM), lambda bi, i: (layer, 0, 0)),
            _const_spec((1, 1, ATT_V_DIM), lambda bi, i: (layer, 0, 0)),
        ],
        out_specs=pl.BlockSpec((1, TQ, ATT_WIDTH), lambda bi, i: (bi, i, 0)),
        out_shape=jax.ShapeDtypeStruct((b, s, ATT_WIDTH), jnp.bfloat16),
        scratch_shapes=[
            pltpu.VMEM((nh, 2 * TQ, ATT_V_DIM), jnp.float32),
            pltpu.VMEM((nh, 2 * TQ, 2 * ATT_V_DIM), jnp.float32),
        ],
        compiler_params=pltpu.CompilerParams(
            dimension_semantics=("arbitrary", "arbitrary"), vmem_limit_bytes=VMEM_LIMIT),
        name="diff_attention",
    )(rel_bias, qkv, qkv, qkv, bias_blocks, att_lambda, att_subnorm_g)


def _causal_conv(buf_ref, halo, cur, w_ref, ktaps):
    tm = cur.shape[0]
    buf_ref[halo:halo + tm, :] = cur
    acc = None
    for k in range(ktaps):
        off = halo - (ktaps - 1) + k
        term = w_ref[0, k:k + 1, :] * buf_ref[off:off + tm, :]
        acc = term if acc is None else acc + term
    buf_ref[0:halo, :] = buf_ref[tm:tm + halo, :]
    return acc


def _linear_scan(a, u, carry):
    tm = a.shape[0]
    row = lax.broadcasted_iota(jnp.int32, a.shape, 0)
    d = 1
    while d < tm:
        a_s = pltpu.roll(a, d, 0)
        u_s = pltpu.roll(u, d, 0)
        keep = row >= d
        u = jnp.where(keep, a * u_s + u, u)
        a = jnp.where(keep, a * a_s, a)
        d *= 2
    return a * carry + u


def _mix_kernel(x_ref, yatt_ref, gpre_ref, w_ref, wbd_ref, bbd_ref, lcw_ref, lcb_ref, llam_ref,
                scw_ref, cfw_ref, cfb_ref, cflg_ref, cflb_ref, gb_ref, wbr_ref, wo_ref, gpost_ref,
                o_ref, lbuf, sbuf, cbuf, hcar):
    tm = TM_MIX
    bw = BRANCH_WIDTH

    @pl.when(pl.program_id(1) == 0)
    def _():
        lbuf[0:LRU_HALO, :] = jnp.zeros((LRU_HALO, bw), jnp.float32)
        sbuf[0:SC_HALO, :] = jnp.zeros((SC_HALO, bw), jnp.float32)
        cbuf[0:CF_HALO, :] = jnp.zeros((CF_HALO, bw), jnp.float32)
        hcar[...] = jnp.zeros(hcar.shape, jnp.float32)

    x = x_ref[0]
    hb = _bf16(_rmsnorm(x, gpre_ref[0]))

    def proj(idx, width=bw):
        c0 = idx * bw
        return _dot(hb, w_ref[0, :, c0:c0 + width])

    def gate(jb):
        c0 = 7 * bw + jb * D_MODEL
        return jax.nn.sigmoid(_dot(hb, w_ref[0, :, c0:c0 + D_MODEL]) + gb_ref[0, jb:jb + 1, :])

    def branch(jb, y):
        return gate(jb) * _dot(_bf16(y), wbr_ref[0, jb])

    merged = branch(0, yatt_ref[0])

    xc = _causal_conv(lbuf, LRU_HALO, proj(0), lcw_ref, LRU_CONV) + lcb_ref[0]
    ri = jax.nn.sigmoid(_dot(_bf16(xc), wbd_ref[0]) + bbd_ref[0])
    r = ri[:, :bw]
    ig = ri[:, bw:]
    log_a = -LRU_C * r * jax.nn.softplus(-llam_ref[0])
    a = jnp.exp(log_a)
    mult = jnp.sqrt(-jnp.tanh(log_a) * (a * a + 1.0))
    hs = _linear_scan(a, mult * (ig * xc), hcar[0:1, :])
    hcar[0:1, :] = hs[tm - 1:tm, :]
    merged = merged + branch(1, hs * jax.nn.gelu(proj(1)))

    sc_b = proj(2)
    cx = proj(3) * proj(4)
    merged = merged + branch(2, sc_b * _causal_conv(sbuf, SC_HALO, cx, scw_ref, SC_CONV))

    u = proj(5) * jax.nn.sigmoid(proj(6))
    u = _causal_conv(cbuf, CF_HALO, u, cfw_ref, CF_CONV) + cfb_ref[0]
    mu = jnp.mean(u, axis=-1, keepdims=True)
    uc = u - mu
    var = jnp.mean(uc * uc, axis=-1, keepdims=True)
    ln = uc * lax.rsqrt(var + EPS) * cflg_ref[0] + cflb_ref[0]
    merged = merged + branch(3, ln * jax.nn.sigmoid(ln))

    o = _dot(_bf16(merged), wo_ref[0])
    o_ref[0] = x + _rmsnorm(o, gpost_ref[0])


def _mixers(x, yatt, layer, gpre, w_rest, wbd, bbd, lcw, lcb, llam, scw, cfw, cfb, cflg, cflb,
            gate_b, w_branch, w_o, gpost):
    b, s, d = x.shape
    bw = BRANCH_WIDTH

    def lspec(*shape):
        nd = len(shape)
        return _const_spec((1,) + shape, lambda bi, i: (layer,) + (0,) * nd)

    return pl.pallas_call(
        _mix_kernel,
        grid=(b, s // TM_MIX),
        in_specs=[
            pl.BlockSpec((1, TM_MIX, d), lambda bi, i: (bi, i, 0)),
            pl.BlockSpec((1, TM_MIX, ATT_WIDTH), lambda bi, i: (bi, i, 0)),
            lspec(1, d),
            lspec(d, REST_WIDTH),
            lspec(bw, 2 * bw),
            lspec(1, 2 * bw),
            lspec(LRU_CONV, bw),
            lspec(1, bw),
            lspec(1, bw),
            lspec(SC_CONV, bw),
            lspec(CF_CONV, bw),
            lspec(1, bw),
            lspec(1, bw),
            lspec(1, bw),
            lspec(4, d),
            lspec(4, bw, d),
            lspec(d, d),
            lspec(1, d),
        ],
        out_specs=pl.BlockSpec((1, TM_MIX, d), lambda bi, i: (bi, i, 0)),
        out_shape=jax.ShapeDtypeStruct((b, s, d), jnp.float32),
        scratch_shapes=[
            pltpu.VMEM((LRU_HALO + TM_MIX, bw), jnp.float32),
            pltpu.VMEM((SC_HALO + TM_MIX, bw), jnp.float32),
            pltpu.VMEM((CF_HALO + TM_MIX, bw), jnp.float32),
            pltpu.VMEM((8, bw), jnp.float32),
        ],
        compiler_params=pltpu.CompilerParams(
            dimension_semantics=("arbitrary", "arbitrary"), vmem_limit_bytes=VMEM_LIMIT),
        name="mixers_merge",
    )(x, yatt, gpre, w_rest, wbd, bbd, lcw, lcb, llam, scw, cfw, cfb, cflg, cflb, gate_b,
      w_branch, w_o, gpost)


def _ffn_kernel(x_ref, p_ref, gpre_ref, wg_ref, wu_ref, wout_ref, gpost_ref, wple_ref, gple_ref,
                wgate_ref, o_ref):
    x = x_ref[...]
    hb = _bf16(_rmsnorm(x, gpre_ref[0]))
    gt = _dot(hb, wg_ref[0])
    up = _dot(hb, wu_ref[0])
    act = _bf16(gt * jax.nn.sigmoid(gt) * up)
    x = x + _rmsnorm(_dot(act, wout_ref[0]), gpost_ref[0])
    e = _dot(_bf16(p_ref[0]), wple_ref[0])
    ge = jax.nn.sigmoid(_dot(_bf16(_rmsnorm(x, gple_ref[0])), wgate_ref[0]))
    o_ref[...] = x + ge * e


def _ffn_ple(x2d, p3d, layer, gpre, w_gate_in, w_up_in, w_out, gpost, w_ple, g_ple, w_ple_gate):
    t, d = x2d.shape

    def lspec(*shape):
        nd = len(shape)
        return _const_spec((1,) + shape, lambda i: (layer,) + (0,) * nd)

    return pl.pallas_call(
        _ffn_kernel,
        grid=(t // TM_FFN,),
        in_specs=[
            pl.BlockSpec((TM_FFN, d), lambda i: (i, 0)),
            pl.BlockSpec((1, TM_FFN, PLE_DIM), lambda i: (layer, i, 0)),
            lspec(1, d),
            lspec(d, FFN_HIDDEN),
            lspec(d, FFN_HIDDEN),
            lspec(FFN_HIDDEN, d),
            lspec(1, d),
            lspec(PLE_DIM, d),
            lspec(1, d),
            lspec(d, d),
        ],
        out_specs=pl.BlockSpec((TM_FFN, d), lambda i: (i, 0)),
        out_shape=jax.ShapeDtypeStruct((t, d), jnp.float32),
        compiler_params=pltpu.CompilerParams(
            dimension_semantics=("arbitrary",), vmem_limit_bytes=VMEM_LIMIT),
        name="ffn_ple",
    )(x2d, p3d, gpre, w_gate_in, w_up_in, w_out, gpost, w_ple, g_ple, w_ple_gate)


def _block_diag(w):
    l, h, i, j = w.shape
    eye = jnp.eye(h, dtype=w.dtype)
    return (w[:, :, :, None, :] * eye[None, :, None, :, None]).reshape(l, h * i, h * j)


def kernel(x, p, rel_bias, g_pre_mix, w_in, att_lambda, att_subnorm_g, lru_conv_w, lru_conv_b, lru_wa, lru_ba, lru_wx, lru_bx, lru_lambda, sc_conv_w, cf_conv_w, cf_conv_b, cf_ln_g, cf_ln_b, gate_b, w_branch, w_o, g_post_mix, g_pre_ffn, w_ffn_in, w_ffn_out, g_post_ffn, w_ple_in, g_ple, w_ple_gate):
    b, s, d = x.shape
    depth = w_in.shape[0]
    t = b * s
    bf = jnp.bfloat16

    def row(v):
        return v[:, None, :]

    w_qkv = w_in[:, :, :QKV_WIDTH].astype(bf)
    w_rest = w_in[:, :, QKV_WIDTH:].astype(bf)
    wbd = jnp.concatenate([_block_diag(lru_wa), _block_diag(lru_wx)], axis=-1).astype(bf)
    bbd = jnp.concatenate([lru_ba.reshape(depth, 1, LRU_WIDTH), lru_bx.reshape(depth, 1, LRU_WIDTH)], axis=-1)
    w_branch_b = w_branch.astype(bf)
    w_o_b = w_o.astype(bf)
    w_gate_in = w_ffn_in[:, :, :FFN_HIDDEN].astype(bf)
    w_up_in = w_ffn_in[:, :, FFN_HIDDEN:].astype(bf)
    w_out_b = w_ffn_out.astype(bf)
    w_ple_b = w_ple_in.astype(bf)
    w_ple_gate_b = w_ple_gate.astype(bf)
    p3d = p.reshape(depth, t, PLE_DIM)

    bias_blocks = _bias_blocks(rel_bias)

    for layer in range(depth):
        lam_init = 0.8 - 0.6 * math.exp(-0.3 * layer)
        qkv = _qkv_proj(x.reshape(t, d), row(g_pre_mix), w_qkv, layer)
        yatt = _diff_attention(qkv.reshape(b, s, QKV_WIDTH), rel_bias, bias_blocks, att_lambda,
                               row(att_subnorm_g), layer, lam_init)
        x = _mixers(x, yatt, layer, row(g_pre_mix), w_rest, wbd, bbd, lru_conv_w, row(lru_conv_b),
                    row(lru_lambda), sc_conv_w, cf_conv_w, row(cf_conv_b), row(cf_ln_g), row(cf_ln_b),
                    gate_b, w_branch_b, w_o_b, row(g_post_mix))
        x = _ffn_ple(x.reshape(t, d), p3d, layer, row(g_pre_ffn), w_gate_in, w_up_in, w_out_b,
                     row(g_post_ffn), w_ple_b, row(g_ple), w_ple_gate_b).reshape(b, s, d)
    return x
```

```python
import functools
import math

import numpy as np
import jax
import jax.numpy as jnp
from jax import lax
from jax.experimental import pallas as pl
from jax.experimental.pallas import tpu as pltpu

D_MODEL = 1024
ATT_HEADS = 4
ATT_QK_DIM = 64
ATT_V_DIM = 128
ATT_WIDTH = 512
REL_BUCKETS = 32
REL_MAX_DIST = 128
NEG_LOGIT = -1e30
LRU_WIDTH = 512
LRU_BLOCKS = 8
LRU_CONV = 4
LRU_C = 8.0
SC_CONV = 3
CF_CONV = 31
BRANCH_WIDTH = 512
FFN_HIDDEN = 2816
PLE_DIM = 256
EPS = 1e-6
LOG2E = math.log2(math.e)
QKV_WIDTH = 3 * ATT_WIDTH
REST_WIDTH = 7 * BRANCH_WIDTH + 4 * D_MODEL

TM_PROJ = 512
TQ = 256
TK = 256
TK_FAR = 2 * TK
TM_MIX = 512
TM_FFN = 512
LRU_HALO = 8
SC_HALO = 8
CF_HALO = 32
VMEM_LIMIT = 56 * 1024 * 1024


def _t5_large_thresholds():
    max_exact = REL_BUCKETS // 2
    n = np.arange(1, 4 * REL_MAX_DIST, dtype=np.float64)
    large = max_exact + (np.log(n / max_exact) / math.log(REL_MAX_DIST / max_exact)
                         * (REL_BUCKETS - max_exact)).astype(np.int64)
    bucket = np.where(n < max_exact, n.astype(np.int64), np.minimum(large, REL_BUCKETS - 1))
    return [int(np.argmax(bucket >= b)) + 1 for b in range(max_exact + 1, REL_BUCKETS)]


T5_THRESHOLDS = _t5_large_thresholds()
assert T5_THRESHOLDS[-1] <= TK + 1


def _const_spec(block_shape, index_map):
    return pl.BlockSpec(block_shape, index_map, pipeline_mode=pl.Buffered(1))


def _rmsnorm(x, g):
    return x * lax.rsqrt(jnp.mean(x * x, axis=-1, keepdims=True) + EPS) * g


def _bf16(x):
    return x.astype(jnp.bfloat16)


def _dot(a, b):
    return jnp.dot(a, b, preferred_element_type=jnp.float32)


def _bias_kernel(table_ref, o_ref):
    h = pl.program_id(0)
    d = pl.program_id(1)
    i = lax.broadcasted_iota(jnp.int32, (TQ, TK), 0)
    j = lax.broadcasted_iota(jnp.int32, (TQ, TK), 1)
    n = jnp.maximum(d * TK + i - j, 0)
    max_exact = REL_BUCKETS // 2
    large = jnp.full((TQ, TK), max_exact, jnp.int32)
    for thr in T5_THRESHOLDS:
        large = large + (n >= thr).astype(jnp.int32)
    bucket = jnp.where(n < max_exact, n, large)
    val = jnp.zeros((TQ, TK), jnp.float32)
    for b in range(REL_BUCKETS):
        val = jnp.where(bucket == b, table_ref[b, h], val)
    o_ref[0, 0] = val * LOG2E


def _bias_blocks(rel_bias):
    return pl.pallas_call(
        _bias_kernel,
        grid=(ATT_HEADS, 2),
        in_specs=[pl.BlockSpec(memory_space=pltpu.SMEM)],
        out_specs=pl.BlockSpec((1, 1, TQ, TK), lambda h, d: (h, d, 0, 0)),
        out_shape=jax.ShapeDtypeStruct((ATT_HEADS, 2, TQ, TK), jnp.float32),
        name="t5_bias_blocks",
    )(rel_bias)


def _qkv_kernel(x_ref, g_ref, w_ref, o_ref):
    hb = _bf16(_rmsnorm(x_ref[...], g_ref[0]))
    z = _dot(hb, w_ref[0])
    o_ref[:, :ATT_WIDTH] = _bf16(z[:, :ATT_WIDTH] * (ATT_QK_DIM ** -0.5 * LOG2E))
    o_ref[:, ATT_WIDTH:] = _bf16(z[:, ATT_WIDTH:])


def _qkv_proj(x2d, g, w_qkv, layer):
    t = x2d.shape[0]
    return pl.pallas_call(
        _qkv_kernel,
        grid=(t // TM_PROJ,),
        in_specs=[
            pl.BlockSpec((TM_PROJ, D_MODEL), lambda i: (i, 0)),
            _const_spec((1, 1, D_MODEL), lambda i: (layer, 0, 0)),
            _const_spec((1, D_MODEL, QKV_WIDTH), lambda i: (layer, 0, 0)),
        ],
        out_specs=pl.BlockSpec((TM_PROJ, QKV_WIDTH), lambda i: (i, 0)),
        out_shape=jax.ShapeDtypeStruct((t, QKV_WIDTH), jnp.bfloat16),
        compiler_params=pltpu.CompilerParams(
            dimension_semantics=("arbitrary",), vmem_limit_bytes=VMEM_LIMIT),
        name="qkv_proj",
    )(x2d, g, w_qkv)


def _attn_kernel(lam_init, table_ref, q_ref, k_ref, v_ref, bias_ref, lam_ref, subg_ref, o_ref,
                 m_sc, acc_sc):
    i = pl.program_id(1)
    rows = 2 * TQ
    dv = ATT_V_DIM
    lane = lax.broadcasted_iota(jnp.int32, (TQ, dv), 1)

    qs = []
    for h in range(ATT_HEADS):
        q = q_ref[0, :, h * dv:(h + 1) * dv]
        zero = jnp.zeros_like(q)
        qs.append(jnp.concatenate([jnp.where(lane < ATT_QK_DIM, q, zero),
                                   jnp.where(lane >= ATT_QK_DIM, q, zero)], axis=0))

    m_sc[...] = jnp.full(m_sc.shape, NEG_LOGIT, jnp.float32)
    acc_sc[...] = jnp.zeros(acc_sc.shape, jnp.float32)

    def block(h, start, width, add_bias):
        k = k_ref[0, pl.ds(start, width), h * dv:(h + 1) * dv]
        v = v_ref[0, pl.ds(start, width), h * dv:(h + 1) * dv]
        s = lax.dot_general(qs[h], k, (((1,), (1,)), ((), ())), preferred_element_type=jnp.float32)
        if add_bias is not None:
            s = add_bias(s)
        m_prev = m_sc[h]
        m_new = jnp.maximum(m_prev, jnp.max(s, axis=1, keepdims=True))
        alpha = jnp.exp2(m_prev - m_new)
        p = jnp.exp2(s - jnp.concatenate([m_new] * (width // dv), axis=1))
        ones = jnp.ones((width, dv), jnp.bfloat16)
        pv = _dot(_bf16(p), jnp.concatenate([v, ones], axis=1))
        acc_sc[h] = jnp.concatenate([alpha, alpha], axis=1) * acc_sc[h] + pv
        m_sc[h] = m_new

    n_far = jnp.maximum(i - 1, 0)
    n_wide = lax.shift_right_logical(n_far, 1)

    def far_body(j, carry):
        for h in range(ATT_HEADS):
            block(h, pl.multiple_of(j * TK_FAR, TK_FAR), TK_FAR, None)
        return carry

    lax.fori_loop(0, n_wide, far_body, 0)

    @pl.when((n_far & 1) == 1)
    def _():
        for h in range(ATT_HEADS):
            block(h, pl.multiple_of(n_wide * TK_FAR, TK), TK, None)

    for h in range(ATT_HEADS):
        m_sc[h] = m_sc[h] + table_ref[REL_BUCKETS - 1, h] * LOG2E

    @pl.when(i >= 1)
    def _():
        for h in range(ATT_HEADS):
            block(h, pl.multiple_of((i - 1) * TK, TK), TK,
                  lambda s: (s.reshape(2, TQ, TK) + bias_ref[h, 1][None]).reshape(rows, TK))

    row = lax.broadcasted_iota(jnp.int32, (TQ, TK), 0)
    col = lax.broadcasted_iota(jnp.int32, (TQ, TK), 1)
    causal = (row >= col)[None]

    def diag_bias(h):
        def f(s):
            s3 = s.reshape(2, TQ, TK) + bias_ref[h, 0][None]
            return jnp.where(causal, s3, NEG_LOGIT).reshape(rows, TK)
        return f

    for h in range(ATT_HEADS):
        block(h, pl.multiple_of(i * TK, TK), TK, diag_bias(h))

    lv = lam_ref[0]
    lam = (jnp.exp(jnp.sum(lv[0:1] * lv[1:2], axis=1, keepdims=True))
           - jnp.exp(jnp.sum(lv[2:3] * lv[3:4], axis=1, keepdims=True)) + lam_init)
    for h in range(ATT_HEADS):
        acc = acc_sc[h]
        o = acc[:, :dv] / acc[:, dv:]
        o = o[:TQ] - lam * o[TQ:]
        o = _rmsnorm(o, subg_ref[0]) * (1.0 - lam_init)
        o_ref[0, :, h * dv:(h + 1) * dv] = _bf16(o)


def _diff_attention(qkv, rel_bias, bias_blocks, att_lambda, att_subnorm_g, layer, lam_init):
    b, s, _ = qkv.shape
    nh = ATT_HEADS
    return pl.pallas_call(
        functools.partial(_attn_kernel, lam_init),
        grid=(b, s // TQ),
        in_specs=[
            pl.BlockSpec(memory_space=pltpu.SMEM),
            pl.BlockSpec((1, TQ, ATT_WIDTH), lambda bi, i: (bi, i, 0)),
            pl.BlockSpec((1, s, ATT_WIDTH), lambda bi, i: (bi, 0, 1)),
            pl.BlockSpec((1, s, ATT_WIDTH), lambda bi, i: (bi, 0, 2)),
            _const_spec((nh, 2, TQ, TK), lambda bi, i: (0, 0, 0, 0)),
            _const_spec((1, 4, ATT_QK_DIM), lambda bi, i: (layer, 0, 0)),
            _const_spec((1, 1, ATT_V_DIM), lambda bi, i: (layer, 0, 0)),
        ],
        out_specs=pl.BlockSpec((1, TQ, ATT_WIDTH), lambda bi, i: (bi, i, 0)),
        out_shape=jax.ShapeDtypeStruct((b, s, ATT_WIDTH), jnp.bfloat16),
        scratch_shapes=[
            pltpu.VMEM((nh, 2 * TQ, ATT_V_DIM), jnp.float32),
            pltpu.VMEM((nh, 2 * TQ, 2 * ATT_V_DIM), jnp.float32),
        ],
        compiler_params=pltpu.CompilerParams(
            dimension_semantics=("arbitrary", "arbitrary"), vmem_limit_bytes=VMEM_LIMIT),
        name="diff_attention",
    )(rel_bias, qkv, qkv, qkv, bias_blocks, att_lambda, att_subnorm_g)


def _causal_conv(buf_ref, stage_ref, halo, cur, w_ref, ktaps):
    tm = cur.shape[0]
    sub = 8
    buf_ref[halo:halo + tm, :] = cur
    first = halo - (ktaps - 1)
    acc = None
    for phase in range(min(sub, ktaps)):
        lo = first + phase
        taps = range(phase, ktaps, sub)
        if lo % sub == 0:
            src, base = buf_ref, lo
        else:
            span = (len(taps) - 1) * sub + tm
            stage_ref[0:span, :] = buf_ref[lo:lo + span, :]
            src, base = stage_ref, 0
        for n, k in enumerate(taps):
            term = w_ref[0, k:k + 1, :] * src[base + n * sub:base + n * sub + tm, :]
            acc = term if acc is None else acc + term
    buf_ref[0:halo, :] = buf_ref[tm:tm + halo, :]
    return acc


def _linear_scan(a, u, carry):
    tm = a.shape[0]
    row = lax.broadcasted_iota(jnp.int32, a.shape, 0)
    d = 1
    while d < tm:
        a_s = pltpu.roll(a, d, 0)
        u_s = pltpu.roll(u, d, 0)
        keep = row >= d
        u = jnp.where(keep, a * u_s + u, u)
        a = jnp.where(keep, a * a_s, a)
        d *= 2
    return a * carry + u


def _mix_kernel(x_ref, yatt_ref, gpre_ref, w_ref, wbd_ref, bbd_ref, lcw_ref, lcb_ref, llam_ref,
                scw_ref, cfw_ref, cfb_ref, cflg_ref, cflb_ref, gb_ref, wbr_ref, wo_ref, gpost_ref,
                o_ref, lbuf, sbuf, cbuf, stage, hcar):
    tm = TM_MIX
    bw = BRANCH_WIDTH

    @pl.when(pl.program_id(1) == 0)
    def _():
        lbuf[0:LRU_HALO, :] = jnp.zeros((LRU_HALO, bw), jnp.float32)
        sbuf[0:SC_HALO, :] = jnp.zeros((SC_HALO, bw), jnp.float32)
        cbuf[0:CF_HALO, :] = jnp.zeros((CF_HALO, bw), jnp.float32)
        hcar[...] = jnp.zeros(hcar.shape, jnp.float32)

    x = x_ref[0]
    hb = _bf16(_rmsnorm(x, gpre_ref[0]))

    def proj(idx, width=bw):
        c0 = idx * bw
        return _dot(hb, w_ref[0, :, c0:c0 + width])

    def gate(jb):
        c0 = 7 * bw + jb * D_MODEL
        return jax.nn.sigmoid(_dot(hb, w_ref[0, :, c0:c0 + D_MODEL]) + gb_ref[0, jb:jb + 1, :])

    def branch(jb, y):
        return gate(jb) * _dot(_bf16(y), wbr_ref[0, jb])

    merged = branch(0, yatt_ref[0])

    xc = _causal_conv(lbuf, stage, LRU_HALO, proj(0), lcw_ref, LRU_CONV) + lcb_ref[0]
    ri = jax.nn.sigmoid(_dot(_bf16(xc), wbd_ref[0]) + bbd_ref[0])
    r = ri[:, :bw]
    ig = ri[:, bw:]
    log_a = -LRU_C * r * jax.nn.softplus(-llam_ref[0])
    a = jnp.exp(log_a)
    mult = jnp.sqrt(-jnp.tanh(log_a) * (a * a + 1.0))
    hs = _linear_scan(a, mult * (ig * xc), hcar[0:1, :])
    hcar[0:1, :] = hs[tm - 1:tm, :]
    merged = merged + branch(1, hs * jax.nn.gelu(proj(1)))

    sc_b = proj(2)
    cx = proj(3) * proj(4)
    merged = merged + branch(2, sc_b * _causal_conv(sbuf, stage, SC_HALO, cx, scw_ref, SC_CONV))

    u = proj(5) * jax.nn.sigmoid(proj(6))
    u = _causal_conv(cbuf, stage, CF_HALO, u, cfw_ref, CF_CONV) + cfb_ref[0]
    mu = jnp.mean(u, axis=-1, keepdims=True)
    uc = u - mu
    var = jnp.mean(uc * uc, axis=-1, keepdims=True)
    ln = uc * lax.rsqrt(var + EPS) * cflg_ref[0] + cflb_ref[0]
    merged = merged + branch(3, ln * jax.nn.sigmoid(ln))

    o = _dot(_bf16(merged), wo_ref[0])
    o_ref[0] = x + _rmsnorm(o, gpost_ref[0])


def _mixers(x, yatt, layer, gpre, w_rest, wbd, bbd, lcw, lcb, llam, scw, cfw, cfb, cflg, cflb,
            gate_b, w_branch, w_o, gpost):
    b, s, d = x.shape
    bw = BRANCH_WIDTH

    def lspec(*shape):
        nd = len(shape)
        return _const_spec((1,) + shape, lambda bi, i: (layer,) + (0,) * nd)

    return pl.pallas_call(
        _mix_kernel,
        grid=(b, s // TM_MIX),
        in_specs=[
            pl.BlockSpec((1, TM_MIX, d), lambda bi, i: (bi, i, 0)),
            pl.BlockSpec((1, TM_MIX, ATT_WIDTH), lambda bi, i: (bi, i, 0)),
            lspec(1, d),
            lspec(d, REST_WIDTH),
            lspec(bw, 2 * bw),
            lspec(1, 2 * bw),
            lspec(LRU_CONV, bw),
            lspec(1, bw),
            lspec(1, bw),
            lspec(SC_CONV, bw),
            lspec(CF_CONV, bw),
            lspec(1, bw),
            lspec(1, bw),
            lspec(1, bw),
            lspec(4, d),
            lspec(4, bw, d),
            lspec(d, d),
            lspec(1, d),
        ],
        out_specs=pl.BlockSpec((1, TM_MIX, d), lambda bi, i: (bi, i, 0)),
        out_shape=jax.ShapeDtypeStruct((b, s, d), jnp.float32),
        scratch_shapes=[
            pltpu.VMEM((LRU_HALO + TM_MIX, bw), jnp.float32),
            pltpu.VMEM((SC_HALO + TM_MIX, bw), jnp.float32),
            pltpu.VMEM((CF_HALO + TM_MIX, bw), jnp.float32),
            pltpu.VMEM((CF_HALO + TM_MIX, bw), jnp.float32),
            pltpu.VMEM((8, bw), jnp.float32),
        ],
        compiler_params=pltpu.CompilerParams(
            dimension_semantics=("arbitrary", "arbitrary"), vmem_limit_bytes=VMEM_LIMIT),
        name="mixers_merge",
    )(x, yatt, gpre, w_rest, wbd, bbd, lcw, lcb, llam, scw, cfw, cfb, cflg, cflb, gate_b,
      w_branch, w_o, gpost)


def _ffn_kernel(x_ref, p_ref, gpre_ref, wg_ref, wu_ref, wout_ref, gpost_ref, wple_ref, gple_ref,
                wgate_ref, o_ref):
    x = x_ref[...]
    hb = _bf16(_rmsnorm(x, gpre_ref[0]))
    gt = _dot(hb, wg_ref[0])
    up = _dot(hb, wu_ref[0])
    act = _bf16(gt * jax.nn.sigmoid(gt) * up)
    x = x + _rmsnorm(_dot(act, wout_ref[0]), gpost_ref[0])
    e = _dot(_bf16(p_ref[0]), wple_ref[0])
    ge = jax.nn.sigmoid(_dot(_bf16(_rmsnorm(x, gple_ref[0])), wgate_ref[0]))
    o_ref[...] = x + ge * e


def _ffn_ple(x2d, p3d, layer, gpre, w_gate_in, w_up_in, w_out, gpost, w_ple, g_ple, w_ple_gate):
    t, d = x2d.shape

    def lspec(*shape):
        nd = len(shape)
        return _const_spec((1,) + shape, lambda i: (layer,) + (0,) * nd)

    return pl.pallas_call(
        _ffn_kernel,
        grid=(t // TM_FFN,),
        in_specs=[
            pl.BlockSpec((TM_FFN, d), lambda i: (i, 0)),
            pl.BlockSpec((1, TM_FFN, PLE_DIM), lambda i: (layer, i, 0)),
            lspec(1, d),
            lspec(d, FFN_HIDDEN),
            lspec(d, FFN_HIDDEN),
            lspec(FFN_HIDDEN, d),
            lspec(1, d),
            lspec(PLE_DIM, d),
            lspec(1, d),
            lspec(d, d),
        ],
        out_specs=pl.BlockSpec((TM_FFN, d), lambda i: (i, 0)),
        out_shape=jax.ShapeDtypeStruct((t, d), jnp.float32),
        compiler_params=pltpu.CompilerParams(
            dimension_semantics=("arbitrary",), vmem_limit_bytes=VMEM_LIMIT),
        name="ffn_ple",
    )(x2d, p3d, gpre, w_gate_in, w_up_in, w_out, gpost, w_ple, g_ple, w_ple_gate)


def _block_diag(w):
    l, h, i, j = w.shape
    eye = jnp.eye(h, dtype=w.dtype)
    return (w[:, :, :, None, :] * eye[None, :, None, :, None]).reshape(l, h * i, h * j)


def kernel(x, p, rel_bias, g_pre_mix, w_in, att_lambda, att_subnorm_g, lru_conv_w, lru_conv_b, lru_wa, lru_ba, lru_wx, lru_bx, lru_lambda, sc_conv_w, cf_conv_w, cf_conv_b, cf_ln_g, cf_ln_b, gate_b, w_branch, w_o, g_post_mix, g_pre_ffn, w_ffn_in, w_ffn_out, g_post_ffn, w_ple_in, g_ple, w_ple_gate):
    b, s, d = x.shape
    depth = w_in.shape[0]
    t = b * s
    bf = jnp.bfloat16

    def row(v):
        return v[:, None, :]

    w_qkv = w_in[:, :, :QKV_WIDTH].astype(bf)
    w_rest = w_in[:, :, QKV_WIDTH:].astype(bf)
    wbd = jnp.concatenate([_block_diag(lru_wa), _block_diag(lru_wx)], axis=-1).astype(bf)
    bbd = jnp.concatenate([lru_ba.reshape(depth, 1, LRU_WIDTH), lru_bx.reshape(depth, 1, LRU_WIDTH)], axis=-1)
    w_branch_b = w_branch.astype(bf)
    w_o_b = w_o.astype(bf)
    w_gate_in = w_ffn_in[:, :, :FFN_HIDDEN].astype(bf)
    w_up_in = w_ffn_in[:, :, FFN_HIDDEN:].astype(bf)
    w_out_b = w_ffn_out.astype(bf)
    w_ple_b = w_ple_in.astype(bf)
    w_ple_gate_b = w_ple_gate.astype(bf)
    p3d = p.reshape(depth, t, PLE_DIM)

    bias_blocks = _bias_blocks(rel_bias)

    for layer in range(depth):
        lam_init = 0.8 - 0.6 * math.exp(-0.3 * layer)
        qkv = _qkv_proj(x.reshape(t, d), row(g_pre_mix), w_qkv, layer)
        yatt = _diff_attention(qkv.reshape(b, s, QKV_WIDTH), rel_bias, bias_blocks, att_lambda,
                               row(att_subnorm_g), layer, lam_init)
        x = _mixers(x, yatt, layer, row(g_pre_mix), w_rest, wbd, bbd, lru_conv_w, row(lru_conv_b),
                    row(lru_lambda), sc_conv_w, cf_conv_w, row(cf_conv_b), row(cf_ln_g), row(cf_ln_b),
                    gate_b, w_branch_b, w_o_b, row(g_post_mix))
        x = _ffn_ple(x.reshape(t, d), p3d, layer, row(g_pre_ffn), w_gate_in, w_up_in, w_out_b,
                     row(g_post_ffn), w_ple_b, row(g_ple), w_ple_gate_b).reshape(b, s, d)
    return x
```

```python
import functools
import math

import numpy as np
import jax
import jax.numpy as jnp
from jax import lax
from jax.experimental import pallas as pl
from jax.experimental.pallas import tpu as pltpu

D_MODEL = 1024
ATT_HEADS = 4
ATT_QK_DIM = 64
ATT_V_DIM = 128
ATT_WIDTH = 512
REL_BUCKETS = 32
REL_MAX_DIST = 128
NEG_LOGIT = -1e30
LRU_WIDTH = 512
LRU_BLOCKS = 8
LRU_CONV = 4
LRU_C = 8.0
SC_CONV = 3
CF_CONV = 31
BRANCH_WIDTH = 512
FFN_HIDDEN = 2816
PLE_DIM = 256
EPS = 1e-6
LOG2E = math.log2(math.e)
QKV_WIDTH = 3 * ATT_WIDTH
REST_WIDTH = 7 * BRANCH_WIDTH + 4 * D_MODEL

TM_PROJ = 512
TQ = 256
TK = 256
TK_FAR = 2 * TK
TM_MIX = 512
TM_FFN = 512
LRU_HALO = 8
SC_HALO = 8
CF_HALO = 32
VMEM_LIMIT = 56 * 1024 * 1024


def _t5_large_thresholds():
    max_exact = REL_BUCKETS // 2
    n = np.arange(1, 4 * REL_MAX_DIST, dtype=np.float64)
    large = max_exact + (np.log(n / max_exact) / math.log(REL_MAX_DIST / max_exact)
                         * (REL_BUCKETS - max_exact)).astype(np.int64)
    bucket = np.where(n < max_exact, n.astype(np.int64), np.minimum(large, REL_BUCKETS - 1))
    return [int(np.argmax(bucket >= b)) + 1 for b in range(max_exact + 1, REL_BUCKETS)]


T5_THRESHOLDS = _t5_large_thresholds()
assert T5_THRESHOLDS[-1] <= TK + 1


def _const_spec(block_shape, index_map):
    return pl.BlockSpec(block_shape, index_map, pipeline_mode=pl.Buffered(1))


def _rmsnorm(x, g):
    return x * lax.rsqrt(jnp.mean(x * x, axis=-1, keepdims=True) + EPS) * g


def _bf16(x):
    return x.astype(jnp.bfloat16)


def _dot(a, b):
    return jnp.dot(a, b, preferred_element_type=jnp.float32)


def _bias_kernel(table_ref, o_ref):
    h = pl.program_id(0)
    d = pl.program_id(1)
    i = lax.broadcasted_iota(jnp.int32, (TQ, TK), 0)
    j = lax.broadcasted_iota(jnp.int32, (TQ, TK), 1)
    n = jnp.maximum(d * TK + i - j, 0)
    max_exact = REL_BUCKETS // 2
    large = jnp.full((TQ, TK), max_exact, jnp.int32)
    for thr in T5_THRESHOLDS:
        large = large + (n >= thr).astype(jnp.int32)
    bucket = jnp.where(n < max_exact, n, large)
    val = jnp.zeros((TQ, TK), jnp.float32)
    for b in range(REL_BUCKETS):
        val = jnp.where(bucket == b, table_ref[b, h], val)
    o_ref[0, 0] = val * LOG2E


def _bias_blocks(rel_bias):
    return pl.pallas_call(
        _bias_kernel,
        grid=(ATT_HEADS, 2),
        in_specs=[pl.BlockSpec(memory_space=pltpu.SMEM)],
        out_specs=pl.BlockSpec((1, 1, TQ, TK), lambda h, d: (h, d, 0, 0)),
        out_shape=jax.ShapeDtypeStruct((ATT_HEADS, 2, TQ, TK), jnp.float32),
        name="t5_bias_blocks",
    )(rel_bias)


def _qkv_kernel(x_ref, g_ref, w_ref, o_ref):
    hb = _bf16(_rmsnorm(x_ref[...], g_ref[0]))
    z = _dot(hb, w_ref[0])
    o_ref[:, :ATT_WIDTH] = _bf16(z[:, :ATT_WIDTH] * (ATT_QK_DIM ** -0.5 * LOG2E))
    o_ref[:, ATT_WIDTH:] = _bf16(z[:, ATT_WIDTH:])


def _qkv_proj(x2d, g, w_qkv, layer):
    t = x2d.shape[0]
    return pl.pallas_call(
        _qkv_kernel,
        grid=(t // TM_PROJ,),
        in_specs=[
            pl.BlockSpec((TM_PROJ, D_MODEL), lambda i: (i, 0)),
            _const_spec((1, 1, D_MODEL), lambda i: (layer, 0, 0)),
            _const_spec((1, D_MODEL, QKV_WIDTH), lambda i: (layer, 0, 0)),
        ],
        out_specs=pl.BlockSpec((TM_PROJ, QKV_WIDTH), lambda i: (i, 0)),
        out_shape=jax.ShapeDtypeStruct((t, QKV_WIDTH), jnp.bfloat16),
        compiler_params=pltpu.CompilerParams(
            dimension_semantics=("arbitrary",), vmem_limit_bytes=VMEM_LIMIT),
        name="qkv_proj",
    )(x2d, g, w_qkv)


def _attn_kernel(lam_init, table_ref, q_ref, k_ref, v_ref, bias_ref, lam_ref, subg_ref, o_ref,
                 m_sc, acc_sc):
    i = pl.program_id(1)
    rows = 2 * TQ
    dv = ATT_V_DIM
    lane = lax.broadcasted_iota(jnp.int32, (TQ, dv), 1)

    qs = []
    for h in range(ATT_HEADS):
        q = q_ref[0, :, h * dv:(h + 1) * dv]
        zero = jnp.zeros_like(q)
        qs.append(jnp.concatenate([jnp.where(lane < ATT_QK_DIM, q, zero),
                                   jnp.where(lane >= ATT_QK_DIM, q, zero)], axis=0))

    m_sc[...] = jnp.full(m_sc.shape, NEG_LOGIT, jnp.float32)
    acc_sc[...] = jnp.zeros(acc_sc.shape, jnp.float32)

    def scores(h, start, width):
        k = k_ref[0, pl.ds(start, width), h * dv:(h + 1) * dv]
        return lax.dot_general(qs[h], k, (((1,), (1,)), ((), ())), preferred_element_type=jnp.float32)

    def update(h, s, start, width, add_bias):
        v = v_ref[0, pl.ds(start, width), h * dv:(h + 1) * dv]
        if add_bias is not None:
            s = add_bias(s)
        m_prev = m_sc[h]
        m_new = jnp.maximum(m_prev, jnp.max(s, axis=1, keepdims=True))
        alpha = jnp.exp2(m_prev - m_new)
        p = jnp.exp2(_bf16(s - jnp.concatenate([m_new] * (width // dv), axis=1)))
        ones = jnp.ones((width, dv), jnp.bfloat16)
        pv = _dot(p, jnp.concatenate([v, ones], axis=1))
        acc_sc[h] = jnp.concatenate([alpha, alpha], axis=1) * acc_sc[h] + pv
        m_sc[h] = m_new

    def blocks(start, width, bias_of_head):
        s_all = [scores(h, start, width) for h in range(ATT_HEADS)]
        for h in range(ATT_HEADS):
            update(h, s_all[h], start, width, bias_of_head(h) if bias_of_head else None)

    n_far = jnp.maximum(i - 1, 0)
    n_wide = lax.shift_right_logical(n_far, 1)

    def far_body(j, carry):
        blocks(pl.multiple_of(j * TK_FAR, TK_FAR), TK_FAR, None)
        return carry

    lax.fori_loop(0, n_wide, far_body, 0)

    @pl.when((n_far & 1) == 1)
    def _():
        blocks(pl.multiple_of(n_wide * TK_FAR, TK), TK, None)

    for h in range(ATT_HEADS):
        m_sc[h] = m_sc[h] + table_ref[REL_BUCKETS - 1, h] * LOG2E

    row = lax.broadcasted_iota(jnp.int32, (TQ, TK), 0)
    col = lax.broadcasted_iota(jnp.int32, (TQ, TK), 1)
    causal = (row >= col)[None]

    def diag_bias(h):
        def f(s):
            s3 = s.reshape(2, TQ, TK) + bias_ref[h, 0][None]
            return jnp.where(causal, s3, NEG_LOGIT).reshape(rows, TK)
        return f

    def tail_bias(h):
        def f(s):
            prev = (s[:, :TK].reshape(2, TQ, TK) + bias_ref[h, 1][None]).reshape(rows, TK)
            return jnp.concatenate([prev, diag_bias(h)(s[:, TK:])], axis=1)
        return f

    @pl.when(i == 0)
    def _():
        blocks(0, TK, diag_bias)

    @pl.when(i >= 1)
    def _():
        blocks(pl.multiple_of((i - 1) * TK, TK), 2 * TK, tail_bias)

    lv = lam_ref[0]
    lam = (jnp.exp(jnp.sum(lv[0:1] * lv[1:2], axis=1, keepdims=True))
           - jnp.exp(jnp.sum(lv[2:3] * lv[3:4], axis=1, keepdims=True)) + lam_init)
    for h in range(ATT_HEADS):
        acc = acc_sc[h]
        o = acc[:, :dv] / acc[:, dv:]
        o = o[:TQ] - lam * o[TQ:]
        o = _rmsnorm(o, subg_ref[0]) * (1.0 - lam_init)
        o_ref[0, :, h * dv:(h + 1) * dv] = _bf16(o)


def _diff_attention(qkv, rel_bias, bias_blocks, att_lambda, att_subnorm_g, layer, lam_init):
    b, s, _ = qkv.shape
    nh = ATT_HEADS
    return pl.pallas_call(
        functools.partial(_attn_kernel, lam_init),
        grid=(b, s // TQ),
        in_specs=[
            pl.BlockSpec(memory_space=pltpu.SMEM),
            pl.BlockSpec((1, TQ, ATT_WIDTH), lambda bi, i: (bi, i, 0)),
            pl.BlockSpec((1, s, ATT_WIDTH), lambda bi, i: (bi, 0, 1)),
            pl.BlockSpec((1, s, ATT_WIDTH), lambda bi, i: (bi, 0, 2)),
            _const_spec((nh, 2, TQ, TK), lambda bi, i: (0, 0, 0, 0)),
            _const_spec((1, 4, ATT_QK_DIM), lambda bi, i: (layer, 0, 0)),
            _const_spec((1, 1, ATT_V_DIM), lambda bi, i: (layer, 0, 0)),
        ],
        out_specs=pl.BlockSpec((1, TQ, ATT_WIDTH), lambda bi, i: (bi, i, 0)),
        out_shape=jax.ShapeDtypeStruct((b, s, ATT_WIDTH), jnp.bfloat16),
        scratch_shapes=[
            pltpu.VMEM((nh, 2 * TQ, ATT_V_DIM), jnp.float32),
            pltpu.VMEM((nh, 2 * TQ, 2 * ATT_V_DIM), jnp.float32),
        ],
        compiler_params=pltpu.CompilerParams(
            dimension_semantics=("arbitrary", "arbitrary"), vmem_limit_bytes=VMEM_LIMIT),
        name="diff_attention",
    )(rel_bias, qkv, qkv, qkv, bias_blocks, att_lambda, att_subnorm_g)


def _causal_conv(buf_ref, stage_ref, halo, cur, w_ref, ktaps):
    tm = cur.shape[0]
    sub = 8
    buf_ref[halo:halo + tm, :] = cur
    first = halo - (ktaps - 1)
    acc = None
    for phase in range(min(sub, ktaps)):
        lo = first + phase
        taps = range(phase, ktaps, sub)
        if lo % sub == 0:
            src, base = buf_ref, lo
        else:
            span = (len(taps) - 1) * sub + tm
            stage_ref[0:span, :] = buf_ref[lo:lo + span, :]
            src, base = stage_ref, 0
        for n, k in enumerate(taps):
            term = w_ref[0, k:k + 1, :] * src[base + n * sub:base + n * sub + tm, :]
            acc = term if acc is None else acc + term
    buf_ref[0:halo, :] = buf_ref[tm:tm + halo, :]
    return acc


def _linear_scan(a, u, carry):
    tm = a.shape[0]
    row = lax.broadcasted_iota(jnp.int32, a.shape, 0)
    d = 1
    while d < tm:
        a_s = pltpu.roll(a, d, 0)
        u_s = pltpu.roll(u, d, 0)
        keep = row >= d
        u = jnp.where(keep, a * u_s + u, u)
        a = jnp.where(keep, a * a_s, a)
        d *= 2
    return a * carry + u


def _mix_kernel(x_ref, yatt_ref, gpre_ref, w_ref, wbd_ref, bbd_ref, lcw_ref, lcb_ref, llam_ref,
                scw_ref, cfw_ref, cfb_ref, cflg_ref, cflb_ref, gb_ref, wbr_ref, wo_ref, gpost_ref,
                o_ref, lbuf, sbuf, cbuf, stage, hcar):
    tm = TM_MIX
    bw = BRANCH_WIDTH

    @pl.when(pl.program_id(1) == 0)
    def _():
        lbuf[0:LRU_HALO, :] = jnp.zeros((LRU_HALO, bw), jnp.float32)
        sbuf[0:SC_HALO, :] = jnp.zeros((SC_HALO, bw), jnp.float32)
        cbuf[0:CF_HALO, :] = jnp.zeros((CF_HALO, bw), jnp.float32)
        hcar[...] = jnp.zeros(hcar.shape, jnp.float32)

    x = x_ref[0]
    hb = _bf16(_rmsnorm(x, gpre_ref[0]))

    def proj(idx):
        c0 = idx * bw
        return _dot(hb, w_ref[0, :, c0:c0 + bw])

    def gate(jb):
        c0 = 7 * bw + jb * D_MODEL
        return jax.nn.sigmoid(_dot(hb, w_ref[0, :, c0:c0 + D_MODEL]) + gb_ref[0, jb:jb + 1, :])

    def branch(jb, y):
        return gate(jb) * _dot(_bf16(y), wbr_ref[0, jb])

    merged = branch(0, yatt_ref[0])

    xc = _causal_conv(lbuf, stage, LRU_HALO, proj(0), lcw_ref, LRU_CONV) + lcb_ref[0]
    ri = jax.nn.sigmoid(_dot(_bf16(xc), wbd_ref[0]) + bbd_ref[0])
    r = ri[:, :bw]
    ig = ri[:, bw:]
    log_a = -LRU_C * r * jax.nn.softplus(-llam_ref[0])
    a = jnp.exp(log_a)
    mult = jnp.sqrt(-jnp.tanh(log_a) * (a * a + 1.0))
    hs = _linear_scan(a, mult * (ig * xc), hcar[0:1, :])
    hcar[0:1, :] = hs[tm - 1:tm, :]
    merged = merged + branch(1, hs * jax.nn.gelu(proj(1)))

    sc_b = proj(2)
    cx = proj(3) * proj(4)
    merged = merged + branch(2, sc_b * _causal_conv(sbuf, stage, SC_HALO, cx, scw_ref, SC_CONV))

    u = proj(5) * jax.nn.sigmoid(proj(6))
    u = _causal_conv(cbuf, stage, CF_HALO, u, cfw_ref, CF_CONV) + cfb_ref[0]
    mu = jnp.mean(u, axis=-1, keepdims=True)
    uc = u - mu
    var = jnp.mean(uc * uc, axis=-1, keepdims=True)
    ln = uc * lax.rsqrt(var + EPS) * cflg_ref[0] + cflb_ref[0]
    merged = merged + branch(3, ln * jax.nn.sigmoid(ln))

    o = _dot(_bf16(merged), wo_ref[0])
    o_ref[0] = x + _rmsnorm(o, gpost_ref[0])


def _mixers(x, yatt, layer, gpre, w_rest, wbd, bbd, lcw, lcb, llam, scw, cfw, cfb, cflg, cflb,
            gate_b, w_branch, w_o, gpost):
    b, s, d = x.shape
    bw = BRANCH_WIDTH

    def lspec(*shape):
        nd = len(shape)
        return _const_spec((1,) + shape, lambda bi, i: (layer,) + (0,) * nd)

    return pl.pallas_call(
        _mix_kernel,
        grid=(b, s // TM_MIX),
        in_specs=[
            pl.BlockSpec((1, TM_MIX, d), lambda bi, i: (bi, i, 0)),
            pl.BlockSpec((1, TM_MIX, ATT_WIDTH), lambda bi, i: (bi, i, 0)),
            lspec(1, d),
            lspec(d, REST_WIDTH),
            lspec(bw, 2 * bw),
            lspec(1, 2 * bw),
            lspec(LRU_CONV, bw),
            lspec(1, bw),
            lspec(1, bw),
            lspec(SC_CONV, bw),
            lspec(CF_CONV, bw),
            lspec(1, bw),
            lspec(1, bw),
            lspec(1, bw),
            lspec(4, d),
            lspec(4, bw, d),
            lspec(d, d),
            lspec(1, d),
        ],
        out_specs=pl.BlockSpec((1, TM_MIX, d), lambda bi, i: (bi, i, 0)),
        out_shape=jax.ShapeDtypeStruct((b, s, d), jnp.float32),
        scratch_shapes=[
            pltpu.VMEM((LRU_HALO + TM_MIX, bw), jnp.float32),
            pltpu.VMEM((SC_HALO + TM_MIX, bw), jnp.float32),
            pltpu.VMEM((CF_HALO + TM_MIX, bw), jnp.float32),
            pltpu.VMEM((CF_HALO + TM_MIX, bw), jnp.float32),
            pltpu.VMEM((8, bw), jnp.float32),
        ],
        compiler_params=pltpu.CompilerParams(
            dimension_semantics=("arbitrary", "arbitrary"), vmem_limit_bytes=VMEM_LIMIT),
        name="mixers_merge",
    )(x, yatt, gpre, w_rest, wbd, bbd, lcw, lcb, llam, scw, cfw, cfb, cflg, cflb, gate_b,
      w_branch, w_o, gpost)


def _ffn_kernel(x_ref, p_ref, gpre_ref, wg_ref, wu_ref, wout_ref, gpost_ref, wple_ref, gple_ref,
                wgate_ref, o_ref):
    x = x_ref[...]
    hb = _bf16(_rmsnorm(x, gpre_ref[0]))
    gt = _dot(hb, wg_ref[0])
    up = _dot(hb, wu_ref[0])
    act = _bf16(gt * jax.nn.sigmoid(gt) * up)
    x = x + _rmsnorm(_dot(act, wout_ref[0]), gpost_ref[0])
    e = _dot(_bf16(p_ref[0]), wple_ref[0])
    ge = jax.nn.sigmoid(_dot(_bf16(_rmsnorm(x, gple_ref[0])), wgate_ref[0]))
    o_ref[...] = x + ge * e


def _ffn_ple(x2d, p3d, layer, gpre, w_gate_in, w_up_in, w_out, gpost, w_ple, g_ple, w_ple_gate):
    t, d = x2d.shape

    def lspec(*shape):
        nd = len(shape)
        return _const_spec((1,) + shape, lambda i: (layer,) + (0,) * nd)

    return pl.pallas_call(
        _ffn_kernel,
        grid=(t // TM_FFN,),
        in_specs=[
            pl.BlockSpec((TM_FFN, d), lambda i: (i, 0)),
            pl.BlockSpec((1, TM_FFN, PLE_DIM), lambda i: (layer, i, 0)),
            lspec(1, d),
            lspec(d, FFN_HIDDEN),
            lspec(d, FFN_HIDDEN),
            lspec(FFN_HIDDEN, d),
            lspec(1, d),
            lspec(PLE_DIM, d),
            lspec(1, d),
            lspec(d, d),
        ],
        out_specs=pl.BlockSpec((TM_FFN, d), lambda i: (i, 0)),
        out_shape=jax.ShapeDtypeStruct((t, d), jnp.float32),
        compiler_params=pltpu.CompilerParams(
            dimension_semantics=("arbitrary",), vmem_limit_bytes=VMEM_LIMIT),
        name="ffn_ple",
    )(x2d, p3d, gpre, w_gate_in, w_up_in, w_out, gpost, w_ple, g_ple, w_ple_gate)


def _block_diag(w):
    l, h, i, j = w.shape
    eye = jnp.eye(h, dtype=w.dtype)
    return (w[:, :, :, None, :] * eye[None, :, None, :, None]).reshape(l, h * i, h * j)


def kernel(x, p, rel_bias, g_pre_mix, w_in, att_lambda, att_subnorm_g, lru_conv_w, lru_conv_b, lru_wa, lru_ba, lru_wx, lru_bx, lru_lambda, sc_conv_w, cf_conv_w, cf_conv_b, cf_ln_g, cf_ln_b, gate_b, w_branch, w_o, g_post_mix, g_pre_ffn, w_ffn_in, w_ffn_out, g_post_ffn, w_ple_in, g_ple, w_ple_gate):
    b, s, d = x.shape
    depth = w_in.shape[0]
    t = b * s
    bf = jnp.bfloat16

    def row(v):
        return v[:, None, :]

    w_qkv = w_in[:, :, :QKV_WIDTH].astype(bf)
    w_rest = w_in[:, :, QKV_WIDTH:].astype(bf)
    wbd = jnp.concatenate([_block_diag(lru_wa), _block_diag(lru_wx)], axis=-1).astype(bf)
    bbd = jnp.concatenate([lru_ba.reshape(depth, 1, LRU_WIDTH), lru_bx.reshape(depth, 1, LRU_WIDTH)], axis=-1)
    w_branch_b = w_branch.astype(bf)
    w_o_b = w_o.astype(bf)
    w_gate_in = w_ffn_in[:, :, :FFN_HIDDEN].astype(bf)
    w_up_in = w_ffn_in[:, :, FFN_HIDDEN:].astype(bf)
    w_out_b = w_ffn_out.astype(bf)
    w_ple_b = w_ple_in.astype(bf)
    w_ple_gate_b = w_ple_gate.astype(bf)
    p3d = p.reshape(depth, t, PLE_DIM)

    bias_blocks = _bias_blocks(rel_bias)

    for layer in range(depth):
        lam_init = 0.8 - 0.6 * math.exp(-0.3 * layer)
        qkv = _qkv_proj(x.reshape(t, d), row(g_pre_mix), w_qkv, layer)
        yatt = _diff_attention(qkv.reshape(b, s, QKV_WIDTH), rel_bias, bias_blocks, att_lambda,
                               row(att_subnorm_g), layer, lam_init)
        x = _mixers(x, yatt, layer, row(g_pre_mix), w_rest, wbd, bbd, lru_conv_w, row(lru_conv_b),
                    row(lru_lambda), sc_conv_w, cf_conv_w, row(cf_conv_b), row(cf_ln_g), row(cf_ln_b),
                    gate_b, w_branch_b, w_o_b, row(g_post_mix))
        x = _ffn_ple(x.reshape(t, d), p3d, layer, row(g_pre_ffn), w_gate_in, w_up_in, w_out_b,
                     row(g_post_ffn), w_ple_b, row(g_ple), w_ple_gate_b).reshape(b, s, d)
    return x
```

```python
import functools
import math

import numpy as np
import jax
import jax.numpy as jnp
from jax import lax
from jax.experimental import pallas as pl
from jax.experimental.pallas import tpu as pltpu

D_MODEL = 1024
ATT_HEADS = 4
ATT_QK_DIM = 64
ATT_V_DIM = 128
ATT_WIDTH = 512
REL_BUCKETS = 32
REL_MAX_DIST = 128
NEG_LOGIT = -1e30
LRU_WIDTH = 512
LRU_BLOCKS = 8
LRU_CONV = 4
LRU_C = 8.0
SC_CONV = 3
CF_CONV = 31
BRANCH_WIDTH = 512
FFN_HIDDEN = 2816
PLE_DIM = 256
EPS = 1e-6
LOG2E = math.log2(math.e)
QKV_WIDTH = 3 * ATT_WIDTH
REST_WIDTH = 7 * BRANCH_WIDTH + 4 * D_MODEL

TM_PROJ = 512
TQ = 256
TK = 256
TK_FAR = 2 * TK
TM_MIX = 256
TM_FFN = 512
SUBLANES = 8
MXU_COLS = 256
VMEM_LIMIT = 56 * 1024 * 1024


def _t5_large_thresholds():
    max_exact = REL_BUCKETS // 2
    n = np.arange(1, 4 * REL_MAX_DIST, dtype=np.float64)
    large = max_exact + (np.log(n / max_exact) / math.log(REL_MAX_DIST / max_exact)
                         * (REL_BUCKETS - max_exact)).astype(np.int64)
    bucket = np.where(n < max_exact, n.astype(np.int64), np.minimum(large, REL_BUCKETS - 1))
    return [int(np.argmax(bucket >= b)) + 1 for b in range(max_exact + 1, REL_BUCKETS)]


T5_THRESHOLDS = _t5_large_thresholds()
assert T5_THRESHOLDS[-1] <= TK + 1


def _const_spec(block_shape, index_map):
    return pl.BlockSpec(block_shape, index_map, pipeline_mode=pl.Buffered(1))


def _rmsnorm(x, g):
    return x * lax.rsqrt(jnp.mean(x * x, axis=-1, keepdims=True) + EPS) * g


def _bf16(x):
    return x.astype(jnp.bfloat16)


def _dot(a, b):
    return jnp.dot(a, b, preferred_element_type=jnp.float32)


def _bias_kernel(table_ref, o_ref):
    h = pl.program_id(0)
    d = pl.program_id(1)
    i = lax.broadcasted_iota(jnp.int32, (TQ, TK), 0)
    j = lax.broadcasted_iota(jnp.int32, (TQ, TK), 1)
    n = jnp.maximum(d * TK + i - j, 0)
    max_exact = REL_BUCKETS // 2
    large = jnp.full((TQ, TK), max_exact, jnp.int32)
    for thr in T5_THRESHOLDS:
        large = large + (n >= thr).astype(jnp.int32)
    bucket = jnp.where(n < max_exact, n, large)
    val = jnp.zeros((TQ, TK), jnp.float32)
    for b in range(REL_BUCKETS):
        val = jnp.where(bucket == b, table_ref[b, h], val)
    o_ref[0, 0] = val * LOG2E


def _bias_blocks(rel_bias):
    return pl.pallas_call(
        _bias_kernel,
        grid=(ATT_HEADS, 2),
        in_specs=[pl.BlockSpec(memory_space=pltpu.SMEM)],
        out_specs=pl.BlockSpec((1, 1, TQ, TK), lambda h, d: (h, d, 0, 0)),
        out_shape=jax.ShapeDtypeStruct((ATT_HEADS, 2, TQ, TK), jnp.float32),
        name="t5_bias_blocks",
    )(rel_bias)


def _qkv_kernel(x_ref, g_ref, w_ref, o_ref):
    hb = _bf16(_rmsnorm(x_ref[...], g_ref[0]))
    z = _dot(hb, w_ref[0])
    o_ref[:, :ATT_WIDTH] = _bf16(z[:, :ATT_WIDTH] * (ATT_QK_DIM ** -0.5 * LOG2E))
    o_ref[:, ATT_WIDTH:] = _bf16(z[:, ATT_WIDTH:])


def _qkv_proj(x2d, g, w_qkv, layer):
    t = x2d.shape[0]
    return pl.pallas_call(
        _qkv_kernel,
        grid=(t // TM_PROJ,),
        in_specs=[
            pl.BlockSpec((TM_PROJ, D_MODEL), lambda i: (i, 0)),
            _const_spec((1, 1, D_MODEL), lambda i: (layer, 0, 0)),
            _const_spec((1, D_MODEL, QKV_WIDTH), lambda i: (layer, 0, 0)),
        ],
        out_specs=pl.BlockSpec((TM_PROJ, QKV_WIDTH), lambda i: (i, 0)),
        out_shape=jax.ShapeDtypeStruct((t, QKV_WIDTH), jnp.bfloat16),
        compiler_params=pltpu.CompilerParams(
            dimension_semantics=("arbitrary",), vmem_limit_bytes=VMEM_LIMIT),
        name="qkv_proj",
    )(x2d, g, w_qkv)


def _attn_kernel(lam_init, table_ref, q_ref, k_ref, v_ref, bias_ref, lam_ref, subg_ref, o_ref,
                 m_sc, acc_sc):
    i = pl.program_id(1)
    rows = 2 * TQ
    dv = ATT_V_DIM
    lane = lax.broadcasted_iota(jnp.int32, (TQ, dv), 1)

    qs = []
    for h in range(ATT_HEADS):
        q = q_ref[0, :, h * dv:(h + 1) * dv]
        zero = jnp.zeros_like(q)
        qs.append(jnp.concatenate([jnp.where(lane < ATT_QK_DIM, q, zero),
                                   jnp.where(lane >= ATT_QK_DIM, q, zero)], axis=0))

    m_sc[...] = jnp.full(m_sc.shape, NEG_LOGIT, jnp.float32)
    acc_sc[...] = jnp.zeros(acc_sc.shape, jnp.float32)

    def scores(h, start, width):
        k = k_ref[0, pl.ds(start, width), h * dv:(h + 1) * dv]
        return lax.dot_general(qs[h], k, (((1,), (1,)), ((), ())), preferred_element_type=jnp.float32)

    def update(h, s, start, width, add_bias):
        v = v_ref[0, pl.ds(start, width), h * dv:(h + 1) * dv]
        if add_bias is not None:
            s = add_bias(s)
        m_prev = m_sc[h]
        m_new = jnp.maximum(m_prev, jnp.max(s, axis=1, keepdims=True))
        alpha = jnp.exp2(m_prev - m_new)
        p = jnp.exp2(_bf16(s - jnp.concatenate([m_new] * (width // dv), axis=1)))
        ones = jnp.ones((width, dv), jnp.bfloat16)
        pv = _dot(p, jnp.concatenate([v, ones], axis=1))
        acc_sc[h] = jnp.concatenate([alpha, alpha], axis=1) * acc_sc[h] + pv
        m_sc[h] = m_new

    def blocks(start, width, bias_of_head):
        s_all = [scores(h, start, width) for h in range(ATT_HEADS)]
        for h in range(ATT_HEADS):
            update(h, s_all[h], start, width, bias_of_head(h) if bias_of_head else None)

    n_far = jnp.maximum(i - 1, 0)
    n_wide = lax.shift_right_logical(n_far, 1)

    def far_body(j, carry):
        blocks(pl.multiple_of(j * TK_FAR, TK_FAR), TK_FAR, None)
        return carry

    lax.fori_loop(0, n_wide, far_body, 0)

    @pl.when((n_far & 1) == 1)
    def _():
        blocks(pl.multiple_of(n_wide * TK_FAR, TK), TK, None)

    for h in range(ATT_HEADS):
        m_sc[h] = m_sc[h] + table_ref[REL_BUCKETS - 1, h] * LOG2E

    row = lax.broadcasted_iota(jnp.int32, (TQ, TK), 0)
    col = lax.broadcasted_iota(jnp.int32, (TQ, TK), 1)
    causal = (row >= col)[None]

    def diag_bias(h):
        def f(s):
            s3 = s.reshape(2, TQ, TK) + bias_ref[h, 0][None]
            return jnp.where(causal, s3, NEG_LOGIT).reshape(rows, TK)
        return f

    def tail_bias(h):
        def f(s):
            prev = (s[:, :TK].reshape(2, TQ, TK) + bias_ref[h, 1][None]).reshape(rows, TK)
            return jnp.concatenate([prev, diag_bias(h)(s[:, TK:])], axis=1)
        return f

    @pl.when(i == 0)
    def _():
        blocks(0, TK, diag_bias)

    @pl.when(i >= 1)
    def _():
        blocks(pl.multiple_of((i - 1) * TK, TK), 2 * TK, tail_bias)

    lv = lam_ref[0]
    lam = (jnp.exp(jnp.sum(lv[0:1] * lv[1:2], axis=1, keepdims=True))
           - jnp.exp(jnp.sum(lv[2:3] * lv[3:4], axis=1, keepdims=True)) + lam_init)
    for h in range(ATT_HEADS):
        acc = acc_sc[h]
        o = acc[:, :dv] / acc[:, dv:]
        o = o[:TQ] - lam * o[TQ:]
        o = _rmsnorm(o, subg_ref[0]) * (1.0 - lam_init)
        o_ref[0, :, h * dv:(h + 1) * dv] = _bf16(o)


def _diff_attention(qkv, rel_bias, bias_blocks, att_lambda, att_subnorm_g, layer, lam_init):
    b, s, _ = qkv.shape
    nh = ATT_HEADS
    return pl.pallas_call(
        functools.partial(_attn_kernel, lam_init),
        grid=(b, s // TQ),
        in_specs=[
            pl.BlockSpec(memory_space=pltpu.SMEM),
            pl.BlockSpec((1, TQ, ATT_WIDTH), lambda bi, i: (bi, i, 0)),
            pl.BlockSpec((1, s, ATT_WIDTH), lambda bi, i: (bi, 0, 1)),
            pl.BlockSpec((1, s, ATT_WIDTH), lambda bi, i: (bi, 0, 2)),
            _const_spec((nh, 2, TQ, TK), lambda bi, i: (0, 0, 0, 0)),
            _const_spec((1, 4, ATT_QK_DIM), lambda bi, i: (layer, 0, 0)),
            _const_spec((1, 1, ATT_V_DIM), lambda bi, i: (layer, 0, 0)),
        ],
        out_specs=pl.BlockSpec((1, TQ, ATT_WIDTH), lambda bi, i: (bi, i, 0)),
        out_shape=jax.ShapeDtypeStruct((b, s, ATT_WIDTH), jnp.bfloat16),
        scratch_shapes=[
            pltpu.VMEM((nh, 2 * TQ, ATT_V_DIM), jnp.float32),
            pltpu.VMEM((nh, 2 * TQ, 2 * ATT_V_DIM), jnp.float32),
        ],
        compiler_params=pltpu.CompilerParams(
            dimension_semantics=("arbitrary", "arbitrary"), vmem_limit_bytes=VMEM_LIMIT),
        name="diff_attention",
    )(rel_bias, qkv, qkv, qkv, bias_blocks, att_lambda, att_subnorm_g)


def _strip_perm(tm):
    npos = tm // SUBLANES
    r = lax.broadcasted_iota(jnp.int32, (tm, tm), 0)
    c = lax.broadcasted_iota(jnp.int32, (tm, tm), 1)
    to_strip = jnp.where(c == (r & (SUBLANES - 1)) * npos + (r >> 3), 1.0, 0.0)
    to_time = jnp.where(r == (c & (SUBLANES - 1)) * npos + (c >> 3), 1.0, 0.0)
    return _bf16(to_strip), _bf16(to_time)


def _tiles(v):
    return [v[j * SUBLANES:(j + 1) * SUBLANES, :] for j in range(v.shape[0] // SUBLANES)]


def _strip_conv(x, prev_ref, w_ref, ktaps):
    tm = x.shape[0]
    npos = tm // SUBLANES
    prev = prev_ref[...]
    prev_ref[...] = x
    sub = lax.broadcasted_iota(jnp.int32, x.shape, 0) & (SUBLANES - 1)
    lo = npos - ktaps
    prev_next = jnp.concatenate([prev[SUBLANES:], prev[:SUBLANES]], axis=0)
    y = jnp.where(sub == SUBLANES - 1, prev_next, x)[lo * SUBLANES:]
    back = _tiles(pltpu.roll(y, 1, 0))
    cur = _tiles(x)
    out = []
    for pos in range(npos):
        acc = None
        for k in range(ktaps):
            shift = ktaps - 1 - k
            src = cur[pos - shift] if pos >= shift else back[pos - shift + npos - lo]
            term = w_ref[0, k:k + 1, :] * src
            acc = term if acc is None else acc + term
        out.append(acc)
    return jnp.concatenate(out, axis=0)


def _strip_scan(a, u, carry_ref):
    at, ut = _tiles(a), _tiles(u)
    h, p = [ut[0]], [at[0]]
    for j in range(1, len(at)):
        h.append(at[j] * h[-1] + ut[j])
        p.append(at[j] * p[-1])
    e, f = h[-1], p[-1]
    sub = lax.broadcasted_iota(jnp.int32, e.shape, 0)
    d = 1
    while d < SUBLANES:
        e_s = pltpu.roll(e, d, 0)
        f_s = pltpu.roll(f, d, 0)
        keep = sub >= d
        e = jnp.where(keep, f * e_s + e, e)
        f = jnp.where(keep, f * f_s, f)
        d *= 2
    c0 = carry_ref[0:1, :]
    ends = e + f * c0
    carry_in = jnp.where(sub == 0, c0, pltpu.roll(ends, 1, 0))
    carry_ref[0:1, :] = ends[SUBLANES - 1:SUBLANES, :]
    return jnp.concatenate([h[j] + p[j] * carry_in for j in range(len(at))], axis=0)


def _mix_kernel(x_ref, yatt_ref, gpre_ref, w_ref, wbd_ref, bbd_ref, lcw_ref, lcb_ref, llam_ref,
                scw_ref, cfw_ref, cfb_ref, cflg_ref, cflb_ref, gb_ref, wbr_ref, wo_ref, gpost_ref,
                o_ref, lprev, sprev, cprev, hcar):
    bw = BRANCH_WIDTH

    @pl.when(pl.program_id(1) == 0)
    def _():
        lprev[...] = jnp.zeros(lprev.shape, jnp.float32)
        sprev[...] = jnp.zeros(sprev.shape, jnp.float32)
        cprev[...] = jnp.zeros(cprev.shape, jnp.float32)
        hcar[...] = jnp.zeros(hcar.shape, jnp.float32)

    to_strip, to_time = _strip_perm(TM_MIX)
    x = x_ref[0]
    hb = _bf16(_dot(to_strip, _bf16(_rmsnorm(x, gpre_ref[0]))))
    yatt = _bf16(_dot(to_strip, yatt_ref[0]))

    def in_proj(c0, width):
        return jnp.concatenate([_dot(hb, w_ref[0, :, c:c + MXU_COLS])
                                for c in range(c0, c0 + width, MXU_COLS)], axis=1)

    def proj(idx):
        return in_proj(idx * bw, bw)

    def gate(jb):
        return jax.nn.sigmoid(in_proj(7 * bw + jb * D_MODEL, D_MODEL) + gb_ref[0, jb:jb + 1, :])

    u = proj(5) * jax.nn.sigmoid(proj(6))
    lru_x = proj(0)
    cx = proj(3) * proj(4)
    sc_b = proj(2)
    lru_g = jax.nn.gelu(proj(1))

    xc = _strip_conv(lru_x, lprev, lcw_ref, LRU_CONV) + lcb_ref[0]
    ri = jax.nn.sigmoid(_dot(_bf16(xc), wbd_ref[0]) + bbd_ref[0])
    gates = [gate(jb) for jb in range(4)]
    merged = gates[0] * _dot(yatt, wbr_ref[0, 0])

    r = ri[:, :bw]
    ig = ri[:, bw:]
    log_a = -LRU_C * r * jax.nn.softplus(-llam_ref[0])
    a = jnp.exp(log_a)
    mult = jnp.sqrt(-jnp.tanh(log_a) * (a * a + 1.0))
    y_lru = _strip_scan(a, mult * (ig * xc), hcar) * lru_g

    y_sc = sc_b * _strip_conv(cx, sprev, scw_ref, SC_CONV)

    u = _strip_conv(u, cprev, cfw_ref, CF_CONV) + cfb_ref[0]
    mu = jnp.mean(u, axis=-1, keepdims=True)
    uc = u - mu
    var = jnp.mean(uc * uc, axis=-1, keepdims=True)
    ln = uc * lax.rsqrt(var + EPS) * cflg_ref[0] + cflb_ref[0]
    y_cf = ln * jax.nn.sigmoid(ln)

    for jb, y in ((1, y_lru), (2, y_sc), (3, y_cf)):
        merged = merged + gates[jb] * _dot(_bf16(y), wbr_ref[0, jb])

    merged = _bf16(_dot(to_time, _bf16(merged)))
    o = _dot(merged, wo_ref[0])
    o_ref[0] = x + _rmsnorm(o, gpost_ref[0])


def _mixers(x, yatt, layer, gpre, w_rest, wbd, bbd, lcw, lcb, llam, scw, cfw, cfb, cflg, cflb,
            gate_b, w_branch, w_o, gpost):
    b, s, d = x.shape
    bw = BRANCH_WIDTH

    def lspec(*shape):
        nd = len(shape)
        return _const_spec((1,) + shape, lambda bi, i: (layer,) + (0,) * nd)

    return pl.pallas_call(
        _mix_kernel,
        grid=(b, s // TM_MIX),
        in_specs=[
            pl.BlockSpec((1, TM_MIX, d), lambda bi, i: (bi, i, 0)),
            pl.BlockSpec((1, TM_MIX, ATT_WIDTH), lambda bi, i: (bi, i, 0)),
            lspec(1, d),
            lspec(d, REST_WIDTH),
            lspec(bw, 2 * bw),
            lspec(1, 2 * bw),
            lspec(LRU_CONV, bw),
            lspec(1, bw),
            lspec(1, bw),
            lspec(SC_CONV, bw),
            lspec(CF_CONV, bw),
            lspec(1, bw),
            lspec(1, bw),
            lspec(1, bw),
            lspec(4, d),
            lspec(4, bw, d),
            lspec(d, d),
            lspec(1, d),
        ],
        out_specs=pl.BlockSpec((1, TM_MIX, d), lambda bi, i: (bi, i, 0)),
        out_shape=jax.ShapeDtypeStruct((b, s, d), jnp.float32),
        scratch_shapes=[
            pltpu.VMEM((TM_MIX, bw), jnp.float32),
            pltpu.VMEM((TM_MIX, bw), jnp.float32),
            pltpu.VMEM((TM_MIX, bw), jnp.float32),
            pltpu.VMEM((SUBLANES, bw), jnp.float32),
        ],
        compiler_params=pltpu.CompilerParams(
            dimension_semantics=("arbitrary", "arbitrary"), vmem_limit_bytes=VMEM_LIMIT),
        name="mixers_merge",
    )(x, yatt, gpre, w_rest, wbd, bbd, lcw, lcb, llam, scw, cfw, cfb, cflg, cflb, gate_b,
      w_branch, w_o, gpost)


def _ffn_kernel(x_ref, p_ref, gpre_ref, wg_ref, wu_ref, wout_ref, gpost_ref, wple_ref, gple_ref,
                wgate_ref, o_ref):
    x = x_ref[...]
    hb = _bf16(_rmsnorm(x, gpre_ref[0]))
    gt = _dot(hb, wg_ref[0])
    up = _dot(hb, wu_ref[0])
    act = _bf16(gt * jax.nn.sigmoid(gt) * up)
    x = x + _rmsnorm(_dot(act, wout_ref[0]), gpost_ref[0])
    e = _dot(_bf16(p_ref[0]), wple_ref[0])
    ge = jax.nn.sigmoid(_dot(_bf16(_rmsnorm(x, gple_ref[0])), wgate_ref[0]))
    o_ref[...] = x + ge * e


def _ffn_ple(x2d, p3d, layer, gpre, w_gate_in, w_up_in, w_out, gpost, w_ple, g_ple, w_ple_gate):
    t, d = x2d.shape

    def lspec(*shape):
        nd = len(shape)
        return _const_spec((1,) + shape, lambda i: (layer,) + (0,) * nd)

    return pl.pallas_call(
        _ffn_kernel,
        grid=(t // TM_FFN,),
        in_specs=[
            pl.BlockSpec((TM_FFN, d), lambda i: (i, 0)),
            pl.BlockSpec((1, TM_FFN, PLE_DIM), lambda i: (layer, i, 0)),
            lspec(1, d),
            lspec(d, FFN_HIDDEN),
            lspec(d, FFN_HIDDEN),
            lspec(FFN_HIDDEN, d),
            lspec(1, d),
            lspec(PLE_DIM, d),
            lspec(1, d),
            lspec(d, d),
        ],
        out_specs=pl.BlockSpec((TM_FFN, d), lambda i: (i, 0)),
        out_shape=jax.ShapeDtypeStruct((t, d), jnp.float32),
        compiler_params=pltpu.CompilerParams(
            dimension_semantics=("arbitrary",), vmem_limit_bytes=VMEM_LIMIT),
        name="ffn_ple",
    )(x2d, p3d, gpre, w_gate_in, w_up_in, w_out, gpost, w_ple, g_ple, w_ple_gate)


def _block_diag(w):
    l, h, i, j = w.shape
    eye = jnp.eye(h, dtype=w.dtype)
    return (w[:, :, :, None, :] * eye[None, :, None, :, None]).reshape(l, h * i, h * j)


def kernel(x, p, rel_bias, g_pre_mix, w_in, att_lambda, att_subnorm_g, lru_conv_w, lru_conv_b, lru_wa, lru_ba, lru_wx, lru_bx, lru_lambda, sc_conv_w, cf_conv_w, cf_conv_b, cf_ln_g, cf_ln_b, gate_b, w_branch, w_o, g_post_mix, g_pre_ffn, w_ffn_in, w_ffn_out, g_post_ffn, w_ple_in, g_ple, w_ple_gate):
    b, s, d = x.shape
    depth = w_in.shape[0]
    t = b * s
    bf = jnp.bfloat16

    def row(v):
        return v[:, None, :]

    w_qkv = w_in[:, :, :QKV_WIDTH].astype(bf)
    w_rest = w_in[:, :, QKV_WIDTH:].astype(bf)
    wbd = jnp.concatenate([_block_diag(lru_wa), _block_diag(lru_wx)], axis=-1).astype(bf)
    bbd = jnp.concatenate([lru_ba.reshape(depth, 1, LRU_WIDTH), lru_bx.reshape(depth, 1, LRU_WIDTH)], axis=-1)
    w_branch_b = w_branch.astype(bf)
    w_o_b = w_o.astype(bf)
    w_gate_in = w_ffn_in[:, :, :FFN_HIDDEN].astype(bf)
    w_up_in = w_ffn_in[:, :, FFN_HIDDEN:].astype(bf)
    w_out_b = w_ffn_out.astype(bf)
    w_ple_b = w_ple_in.astype(bf)
    w_ple_gate_b = w_ple_gate.astype(bf)
    p3d = p.reshape(depth, t, PLE_DIM)

    bias_blocks = _bias_blocks(rel_bias)

    for layer in range(depth):
        lam_init = 0.8 - 0.6 * math.exp(-0.3 * layer)
        qkv = _qkv_proj(x.reshape(t, d), row(g_pre_mix), w_qkv, layer)
        yatt = _diff_attention(qkv.reshape(b, s, QKV_WIDTH), rel_bias, bias_blocks, att_lambda,
                               row(att_subnorm_g), layer, lam_init)
        x = _mixers(x, yatt, layer, row(g_pre_mix), w_rest, wbd, bbd, lru_conv_w, row(lru_conv_b),
                    row(lru_lambda), sc_conv_w, cf_conv_w, row(cf_conv_b), row(cf_ln_g), row(cf_ln_b),
                    gate_b, w_branch_b, w_o_b, row(g_post_mix))
        x = _ffn_ple(x.reshape(t, d), p3d, layer, row(g_pre_ffn), w_gate_in, w_up_in, w_out_b,
                     row(g_post_ffn), w_ple_b, row(g_ple), w_ple_gate_b).reshape(b, s, d)
    return x
```

```python
import functools
import math

import numpy as np
import jax
import jax.numpy as jnp
from jax import lax
from jax.experimental import pallas as pl
from jax.experimental.pallas import tpu as pltpu

D_MODEL = 1024
ATT_HEADS = 4
ATT_QK_DIM = 64
ATT_V_DIM = 128
ATT_WIDTH = 512
REL_BUCKETS = 32
REL_MAX_DIST = 128
NEG_LOGIT = -1e30
LRU_WIDTH = 512
LRU_BLOCKS = 8
LRU_CONV = 4
LRU_C = 8.0
SC_CONV = 3
CF_CONV = 31
BRANCH_WIDTH = 512
FFN_HIDDEN = 2816
PLE_DIM = 256
EPS = 1e-6
LOG2E = math.log2(math.e)
QKV_WIDTH = 3 * ATT_WIDTH
GATE_COL0 = 7 * BRANCH_WIDTH
REST_WIDTH = GATE_COL0 + 4 * D_MODEL

TM_PROJ = 512
TQ = 256
TK = 256
TK_FAR = 2 * TK
TM_MIX = 256
TM_FFN = 512
SUBLANES = 8
MXU_COLS = 256
VMEM_LIMIT = 56 * 1024 * 1024


def _t5_large_thresholds():
    max_exact = REL_BUCKETS // 2
    n = np.arange(1, 4 * REL_MAX_DIST, dtype=np.float64)
    large = max_exact + (np.log(n / max_exact) / math.log(REL_MAX_DIST / max_exact)
                         * (REL_BUCKETS - max_exact)).astype(np.int64)
    bucket = np.where(n < max_exact, n.astype(np.int64), np.minimum(large, REL_BUCKETS - 1))
    return [int(np.argmax(bucket >= b)) + 1 for b in range(max_exact + 1, REL_BUCKETS)]


T5_THRESHOLDS = _t5_large_thresholds()
assert T5_THRESHOLDS[-1] <= TK + 1


def _const_spec(block_shape, index_map):
    return pl.BlockSpec(block_shape, index_map, pipeline_mode=pl.Buffered(1))


def _rmsnorm(x, g):
    return x * lax.rsqrt(jnp.mean(x * x, axis=-1, keepdims=True) + EPS) * g


def _bf16(x):
    return x.astype(jnp.bfloat16)


def _dot(a, b):
    return jnp.dot(a, b, preferred_element_type=jnp.float32)


def _bias_kernel(table_ref, o_ref):
    h = pl.program_id(0)
    d = pl.program_id(1)
    i = lax.broadcasted_iota(jnp.int32, (TQ, TK), 0)
    j = lax.broadcasted_iota(jnp.int32, (TQ, TK), 1)
    n = jnp.maximum(d * TK + i - j, 0)
    max_exact = REL_BUCKETS // 2
    large = jnp.full((TQ, TK), max_exact, jnp.int32)
    for thr in T5_THRESHOLDS:
        large = large + (n >= thr).astype(jnp.int32)
    bucket = jnp.where(n < max_exact, n, large)
    val = jnp.zeros((TQ, TK), jnp.float32)
    for b in range(REL_BUCKETS):
        val = jnp.where(bucket == b, table_ref[b, h], val)
    o_ref[0, 0] = val * LOG2E


def _bias_blocks(rel_bias):
    return pl.pallas_call(
        _bias_kernel,
        grid=(ATT_HEADS, 2),
        in_specs=[pl.BlockSpec(memory_space=pltpu.SMEM)],
        out_specs=pl.BlockSpec((1, 1, TQ, TK), lambda h, d: (h, d, 0, 0)),
        out_shape=jax.ShapeDtypeStruct((ATT_HEADS, 2, TQ, TK), jnp.float32),
        name="t5_bias_blocks",
    )(rel_bias)


def _qkv_kernel(x_ref, g_ref, w_ref, o_ref):
    hb = _bf16(_rmsnorm(x_ref[...], g_ref[0]))
    z = _dot(hb, w_ref[0])
    o_ref[:, :ATT_WIDTH] = _bf16(z[:, :ATT_WIDTH] * (ATT_QK_DIM ** -0.5 * LOG2E))
    o_ref[:, ATT_WIDTH:] = _bf16(z[:, ATT_WIDTH:])


def _qkv_proj(x2d, g, w_qkv, layer):
    t = x2d.shape[0]
    return pl.pallas_call(
        _qkv_kernel,
        grid=(t // TM_PROJ,),
        in_specs=[
            pl.BlockSpec((TM_PROJ, D_MODEL), lambda i: (i, 0)),
            _const_spec((1, 1, D_MODEL), lambda i: (layer, 0, 0)),
            _const_spec((1, D_MODEL, QKV_WIDTH), lambda i: (layer, 0, 0)),
        ],
        out_specs=pl.BlockSpec((TM_PROJ, QKV_WIDTH), lambda i: (i, 0)),
        out_shape=jax.ShapeDtypeStruct((t, QKV_WIDTH), jnp.bfloat16),
        compiler_params=pltpu.CompilerParams(
            dimension_semantics=("arbitrary",), vmem_limit_bytes=VMEM_LIMIT),
        name="qkv_proj",
    )(x2d, g, w_qkv)


def _attn_kernel(lam_init, table_ref, q_ref, k_ref, v_ref, bias_ref, lam_ref, subg_ref, o_ref,
                 m_sc, acc_sc):
    i = pl.program_id(1)
    rows = 2 * TQ
    dv = ATT_V_DIM
    lane = lax.broadcasted_iota(jnp.int32, (TQ, dv), 1)

    qs = []
    for h in range(ATT_HEADS):
        q = q_ref[0, :, h * dv:(h + 1) * dv]
        zero = jnp.zeros_like(q)
        qs.append(jnp.concatenate([jnp.where(lane < ATT_QK_DIM, q, zero),
                                   jnp.where(lane >= ATT_QK_DIM, q, zero)], axis=0))

    def scores(h, start, width):
        k = k_ref[0, pl.ds(start, width), h * dv:(h + 1) * dv]
        return lax.dot_general(qs[h], k, (((1,), (1,)), ((), ())), preferred_element_type=jnp.float32)

    def update(h, s, start, width, add_bias, first):
        v = v_ref[0, pl.ds(start, width), h * dv:(h + 1) * dv]
        if add_bias is not None:
            s = add_bias(s)
        m_blk = jnp.max(s, axis=1, keepdims=True)
        m_new = jnp.broadcast_to(m_blk, (rows, dv)) if first else jnp.maximum(m_sc[h], m_blk)
        p = jnp.exp2(_bf16(s - jnp.concatenate([m_new] * (width // dv), axis=1)))
        ones = jnp.ones((width, dv), jnp.bfloat16)
        pv = _dot(p, jnp.concatenate([v, ones], axis=1))
        if first:
            acc_sc[h] = pv
            m_sc[h] = m_new - table_ref[REL_BUCKETS - 1, h] * LOG2E
        else:
            alpha = jnp.exp2(m_sc[h] - m_new)
            acc_sc[h] = jnp.concatenate([alpha, alpha], axis=1) * acc_sc[h] + pv
            m_sc[h] = m_new

    def blocks(start, width, bias_of_head, first=False):
        s_all = [scores(h, start, width) for h in range(ATT_HEADS)]
        for h in range(ATT_HEADS):
            update(h, s_all[h], start, width, bias_of_head(h) if bias_of_head else None, first)

    row = lax.broadcasted_iota(jnp.int32, (TQ, TK), 0)
    col = lax.broadcasted_iota(jnp.int32, (TQ, TK), 1)
    causal = (row >= col)[None]

    def diag_bias(h):
        def f(s):
            s3 = s.reshape(2, TQ, TK) + bias_ref[h, 0][None]
            return jnp.where(causal, s3, NEG_LOGIT).reshape(rows, TK)
        return f

    def tail_bias(h):
        def f(s):
            prev = (s[:, :TK].reshape(2, TQ, TK) + bias_ref[h, 1][None]).reshape(rows, TK)
            return jnp.concatenate([prev, diag_bias(h)(s[:, TK:])], axis=1)
        return f

    @pl.when(i == 0)
    def _():
        blocks(0, TK, diag_bias, first=True)

    @pl.when(i >= 1)
    def _():
        blocks(pl.multiple_of((i - 1) * TK, TK), 2 * TK, tail_bias, first=True)

    n_far = jnp.maximum(i - 1, 0)
    n_wide = lax.shift_right_logical(n_far, 1)

    def far_body(j, carry):
        blocks(pl.multiple_of(j * TK_FAR, TK_FAR), TK_FAR, None)
        return carry

    lax.fori_loop(0, n_wide, far_body, 0)

    @pl.when((n_far & 1) == 1)
    def _():
        blocks(pl.multiple_of(n_wide * TK_FAR, TK), TK, None)

    lv = lam_ref[0]
    lam = (jnp.exp(jnp.sum(lv[0:1] * lv[1:2], axis=1, keepdims=True))
           - jnp.exp(jnp.sum(lv[2:3] * lv[3:4], axis=1, keepdims=True)) + lam_init)
    for h in range(ATT_HEADS):
        acc = acc_sc[h]
        o = acc[:, :dv] / acc[:, dv:]
        o = o[:TQ] - lam * o[TQ:]
        o = _rmsnorm(o, subg_ref[0]) * (1.0 - lam_init)
        o_ref[0, :, h * dv:(h + 1) * dv] = _bf16(o)


def _diff_attention(qkv, rel_bias, bias_blocks, att_lambda, att_subnorm_g, layer, lam_init):
    b, s, _ = qkv.shape
    nh = ATT_HEADS
    return pl.pallas_call(
        functools.partial(_attn_kernel, lam_init),
        grid=(b, s // TQ),
        in_specs=[
            pl.BlockSpec(memory_space=pltpu.SMEM),
            pl.BlockSpec((1, TQ, ATT_WIDTH), lambda bi, i: (bi, i, 0)),
            pl.BlockSpec((1, s, ATT_WIDTH), lambda bi, i: (bi, 0, 1)),
            pl.BlockSpec((1, s, ATT_WIDTH), lambda bi, i: (bi, 0, 2)),
            _const_spec((nh, 2, TQ, TK), lambda bi, i: (0, 0, 0, 0)),
            _const_spec((1, 4, ATT_QK_DIM), lambda bi, i: (layer, 0, 0)),
            _const_spec((1, 1, ATT_V_DIM), lambda bi, i: (layer, 0, 0)),
        ],
        out_specs=pl.BlockSpec((1, TQ, ATT_WIDTH), lambda bi, i: (bi, i, 0)),
        out_shape=jax.ShapeDtypeStruct((b, s, ATT_WIDTH), jnp.bfloat16),
        scratch_shapes=[
            pltpu.VMEM((nh, 2 * TQ, ATT_V_DIM), jnp.float32),
            pltpu.VMEM((nh, 2 * TQ, 2 * ATT_V_DIM), jnp.float32),
        ],
        compiler_params=pltpu.CompilerParams(
            dimension_semantics=("arbitrary", "arbitrary"), vmem_limit_bytes=VMEM_LIMIT),
        name="diff_attention",
    )(rel_bias, qkv, qkv, qkv, bias_blocks, att_lambda, att_subnorm_g)


def _strip_perm(tm):
    npos = tm // SUBLANES
    r = lax.broadcasted_iota(jnp.int32, (tm, tm), 0)
    c = lax.broadcasted_iota(jnp.int32, (tm, tm), 1)
    to_strip = jnp.where(c == (r & (SUBLANES - 1)) * npos + (r >> 3), 1.0, 0.0)
    to_time = jnp.where(r == (c & (SUBLANES - 1)) * npos + (c >> 3), 1.0, 0.0)
    return _bf16(to_strip), _bf16(to_time)


def _tiles(v):
    return [v[j * SUBLANES:(j + 1) * SUBLANES, :] for j in range(v.shape[0] // SUBLANES)]


def _strip_conv(x, prev_ref, w_ref, ktaps):
    tm = x.shape[0]
    npos = tm // SUBLANES
    prev = prev_ref[...]
    prev_ref[...] = x
    sub = lax.broadcasted_iota(jnp.int32, x.shape, 0) & (SUBLANES - 1)
    lo = npos - ktaps
    prev_next = jnp.concatenate([prev[SUBLANES:], prev[:SUBLANES]], axis=0)
    y = jnp.where(sub == SUBLANES - 1, prev_next, x)[lo * SUBLANES:]
    back = _tiles(pltpu.roll(y, 1, 0))
    cur = _tiles(x)
    out = []
    for pos in range(npos):
        acc = None
        for k in range(ktaps):
            shift = ktaps - 1 - k
            src = cur[pos - shift] if pos >= shift else back[pos - shift + npos - lo]
            term = w_ref[0, k:k + 1, :] * src
            acc = term if acc is None else acc + term
        out.append(acc)
    return jnp.concatenate(out, axis=0)


def _strip_scan(a, u, carry_ref):
    at, ut = _tiles(a), _tiles(u)
    h, p = [ut[0]], [at[0]]
    for j in range(1, len(at)):
        h.append(at[j] * h[-1] + ut[j])
        p.append(at[j] * p[-1])
    e, f = h[-1], p[-1]
    sub = lax.broadcasted_iota(jnp.int32, e.shape, 0)
    d = 1
    while d < SUBLANES:
        e_s = pltpu.roll(e, d, 0)
        f_s = pltpu.roll(f, d, 0)
        keep = sub >= d
        e = jnp.where(keep, f * e_s + e, e)
        f = jnp.where(keep, f * f_s, f)
        d *= 2
    c0 = carry_ref[0:1, :]
    ends = e + f * c0
    carry_in = jnp.where(sub == 0, c0, pltpu.roll(ends, 1, 0))
    carry_ref[0:1, :] = ends[SUBLANES - 1:SUBLANES, :]
    return jnp.concatenate([h[j] + p[j] * carry_in for j in range(len(at))], axis=0)


def _mix_kernel(x_ref, yatt_ref, gpre_ref, w_ref, wbd_ref, bbd_ref, lcw_ref, lcb_ref, llam_ref,
                scw_ref, cfw_ref, cfb_ref, cflg_ref, cflb_ref, gb_ref, wbr_ref, wo_ref, gpost_ref,
                o_ref, lprev, sprev, cprev, hcar):
    bw = BRANCH_WIDTH

    @pl.when(pl.program_id(1) == 0)
    def _():
        lprev[...] = jnp.zeros(lprev.shape, jnp.float32)
        sprev[...] = jnp.zeros(sprev.shape, jnp.float32)
        cprev[...] = jnp.zeros(cprev.shape, jnp.float32)
        hcar[...] = jnp.zeros(hcar.shape, jnp.float32)

    to_strip, to_time = _strip_perm(TM_MIX)
    x = x_ref[0]
    hb = _bf16(_dot(to_strip, _bf16(_rmsnorm(x, gpre_ref[0]))))
    yatt = _bf16(_dot(to_strip, yatt_ref[0]))

    def proj(idx):
        c0 = idx * bw
        return _dot(hb, w_ref[0, :, c0:c0 + bw])

    def gate(jb):
        c0 = GATE_COL0 + jb * D_MODEL
        return jnp.tanh(_dot(hb, w_ref[0, :, c0:c0 + D_MODEL]) + gb_ref[0, jb:jb + 1, :]) + 1.0

    u = proj(5) * jax.nn.sigmoid(proj(6))
    lru_x = proj(0)
    cx = proj(3) * proj(4)
    sc_b = proj(2)
    lru_g = jax.nn.gelu(proj(1))

    xc = _strip_conv(lru_x, lprev, lcw_ref, LRU_CONV) + lcb_ref[0]
    ri = jax.nn.sigmoid(_dot(_bf16(xc), wbd_ref[0]) + bbd_ref[0])
    gates = [gate(jb) for jb in range(4)]
    merged = gates[0] * _dot(yatt, wbr_ref[0, 0])

    r = ri[:, :bw]
    ig = ri[:, bw:]
    log_a = -LRU_C * r * jax.nn.softplus(-llam_ref[0])
    a = jnp.exp(log_a)
    sq = -jnp.tanh(log_a) * (a * a + 1.0)
    mult = jnp.where(sq > 0.0, sq * lax.rsqrt(sq), 0.0)
    y_lru = _strip_scan(a, mult * (ig * xc), hcar) * lru_g

    y_sc = sc_b * _strip_conv(cx, sprev, scw_ref, SC_CONV)

    u = _strip_conv(u, cprev, cfw_ref, CF_CONV) + cfb_ref[0]
    mu = jnp.mean(u, axis=-1, keepdims=True)
    uc = u - mu
    var = jnp.mean(uc * uc, axis=-1, keepdims=True)
    ln = uc * lax.rsqrt(var + EPS) * cflg_ref[0] + cflb_ref[0]
    y_cf = ln * jax.nn.sigmoid(ln)

    for jb, y in ((1, y_lru), (2, y_sc), (3, y_cf)):
        merged = merged + gates[jb] * _dot(_bf16(y), wbr_ref[0, jb])

    merged = _bf16(_dot(to_time, _bf16(merged)))
    o = _dot(merged, wo_ref[0])
    o_ref[0] = x + _rmsnorm(o, gpost_ref[0])


def _mixers(x, yatt, layer, gpre, w_rest, wbd, bbd, lcw, lcb, llam, scw, cfw, cfb, cflg, cflb,
            gate_b, w_branch, w_o, gpost):
    b, s, d = x.shape
    bw = BRANCH_WIDTH

    def lspec(*shape):
        nd = len(shape)
        return _const_spec((1,) + shape, lambda bi, i: (layer,) + (0,) * nd)

    return pl.pallas_call(
        _mix_kernel,
        grid=(b, s // TM_MIX),
        in_specs=[
            pl.BlockSpec((1, TM_MIX, d), lambda bi, i: (bi, i, 0)),
            pl.BlockSpec((1, TM_MIX, ATT_WIDTH), lambda bi, i: (bi, i, 0)),
            lspec(1, d),
            lspec(d, REST_WIDTH),
            lspec(bw, 2 * bw),
            lspec(1, 2 * bw),
            lspec(LRU_CONV, bw),
            lspec(1, bw),
            lspec(1, bw),
            lspec(SC_CONV, bw),
            lspec(CF_CONV, bw),
            lspec(1, bw),
            lspec(1, bw),
            lspec(1, bw),
            lspec(4, d),
            lspec(4, bw, d),
            lspec(d, d),
            lspec(1, d),
        ],
        out_specs=pl.BlockSpec((1, TM_MIX, d), lambda bi, i: (bi, i, 0)),
        out_shape=jax.ShapeDtypeStruct((b, s, d), jnp.float32),
        scratch_shapes=[
            pltpu.VMEM((TM_MIX, bw), jnp.float32),
            pltpu.VMEM((TM_MIX, bw), jnp.float32),
            pltpu.VMEM((TM_MIX, bw), jnp.float32),
            pltpu.VMEM((SUBLANES, bw), jnp.float32),
        ],
        compiler_params=pltpu.CompilerParams(
            dimension_semantics=("arbitrary", "arbitrary"), vmem_limit_bytes=VMEM_LIMIT),
        name="mixers_merge",
    )(x, yatt, gpre, w_rest, wbd, bbd, lcw, lcb, llam, scw, cfw, cfb, cflg, cflb, gate_b,
      w_branch, w_o, gpost)


def _ffn_kernel(x_ref, p_ref, gpre_ref, win_ref, wout_ref, gpost_ref, wple_ref, gple_ref,
                wgate_ref, o_ref):
    x = x_ref[...]
    hb = _bf16(_rmsnorm(x, gpre_ref[0]))
    gt = _dot(hb, win_ref[0, :, :FFN_HIDDEN])
    up = _dot(hb, win_ref[0, :, FFN_HIDDEN:])
    act = _bf16(gt * jax.nn.sigmoid(gt) * up)
    x = x + _rmsnorm(_dot(act, wout_ref[0]), gpost_ref[0])
    e = _dot(_bf16(p_ref[0]), wple_ref[0])
    ge = jax.nn.sigmoid(_dot(_bf16(_rmsnorm(x, gple_ref[0])), wgate_ref[0]))
    o_ref[...] = x + ge * e


def _ffn_ple(x2d, p3d, layer, gpre, w_ffn_in, w_out, gpost, w_ple, g_ple, w_ple_gate):
    t, d = x2d.shape

    def lspec(*shape):
        nd = len(shape)
        return _const_spec((1,) + shape, lambda i: (layer,) + (0,) * nd)

    return pl.pallas_call(
        _ffn_kernel,
        grid=(t // TM_FFN,),
        in_specs=[
            pl.BlockSpec((TM_FFN, d), lambda i: (i, 0)),
            pl.BlockSpec((1, TM_FFN, PLE_DIM), lambda i: (layer, i, 0)),
            lspec(1, d),
            lspec(d, 2 * FFN_HIDDEN),
            lspec(FFN_HIDDEN, d),
            lspec(1, d),
            lspec(PLE_DIM, d),
            lspec(1, d),
            lspec(d, d),
        ],
        out_specs=pl.BlockSpec((TM_FFN, d), lambda i: (i, 0)),
        out_shape=jax.ShapeDtypeStruct((t, d), jnp.float32),
        compiler_params=pltpu.CompilerParams(
            dimension_semantics=("arbitrary",), vmem_limit_bytes=VMEM_LIMIT),
        name="ffn_ple",
    )(x2d, p3d, gpre, w_ffn_in, w_out, gpost, w_ple, g_ple, w_ple_gate)


def _block_diag(w):
    l, h, i, j = w.shape
    eye = jnp.eye(h, dtype=w.dtype)
    return (w[:, :, :, None, :] * eye[None, :, None, :, None]).reshape(l, h * i, h * j)


def kernel(x, p, rel_bias, g_pre_mix, w_in, att_lambda, att_subnorm_g, lru_conv_w, lru_conv_b, lru_wa, lru_ba, lru_wx, lru_bx, lru_lambda, sc_conv_w, cf_conv_w, cf_conv_b, cf_ln_g, cf_ln_b, gate_b, w_branch, w_o, g_post_mix, g_pre_ffn, w_ffn_in, w_ffn_out, g_post_ffn, w_ple_in, g_ple, w_ple_gate):
    b, s, d = x.shape
    depth = w_in.shape[0]
    t = b * s
    bf = jnp.bfloat16

    def row(v):
        return v[:, None, :]

    w_qkv = w_in[:, :, :QKV_WIDTH].astype(bf)
    col_scale = jnp.where(jnp.arange(REST_WIDTH) >= GATE_COL0, 0.5, 1.0).astype(jnp.float32)
    w_rest = (w_in[:, :, QKV_WIDTH:] * col_scale).astype(bf)
    gate_b_half = gate_b * 0.5
    wbd = jnp.concatenate([_block_diag(lru_wa), _block_diag(lru_wx)], axis=-1).astype(bf)
    bbd = jnp.concatenate([lru_ba.reshape(depth, 1, LRU_WIDTH), lru_bx.reshape(depth, 1, LRU_WIDTH)], axis=-1)
    w_branch_half = (w_branch * 0.5).astype(bf)
    w_o_b = w_o.astype(bf)
    w_ffn_in_b = w_ffn_in.astype(bf)
    w_out_b = w_ffn_out.astype(bf)
    w_ple_b = w_ple_in.astype(bf)
    w_ple_gate_b = w_ple_gate.astype(bf)
    p3d = p.reshape(depth, t, PLE_DIM)

    bias_blocks = _bias_blocks(rel_bias)

    for layer in range(depth):
        lam_init = 0.8 - 0.6 * math.exp(-0.3 * layer)
        qkv = _qkv_proj(x.reshape(t, d), row(g_pre_mix), w_qkv, layer)
        yatt = _diff_attention(qkv.reshape(b, s, QKV_WIDTH), rel_bias, bias_blocks, att_lambda,
                               row(att_subnorm_g), layer, lam_init)
        x = _mixers(x, yatt, layer, row(g_pre_mix), w_rest, wbd, bbd, lru_conv_w, row(lru_conv_b),
                    row(lru_lambda), sc_conv_w, cf_conv_w, row(cf_conv_b), row(cf_ln_g), row(cf_ln_b),
                    gate_b_half, w_branch_half, w_o_b, row(g_post_mix))
        x = _ffn_ple(x.reshape(t, d), p3d, layer, row(g_pre_ffn), w_ffn_in_b, w_out_b,
                     row(g_post_ffn), w_ple_b, row(g_ple), w_ple_gate_b).reshape(b, s, d)
    return x
```

```python
import functools
import math

import numpy as np
import jax
import jax.numpy as jnp
from jax import lax
from jax.experimental import pallas as pl
from jax.experimental.pallas import tpu as pltpu

D_MODEL = 1024
ATT_HEADS = 4
ATT_QK_DIM = 64
ATT_V_DIM = 128
ATT_WIDTH = 512
REL_BUCKETS = 32
REL_MAX_DIST = 128
NEG_LOGIT = -1e30
LRU_WIDTH = 512
LRU_BLOCKS = 8
LRU_CONV = 4
LRU_C = 8.0
SC_CONV = 3
CF_CONV = 31
BRANCH_WIDTH = 512
FFN_HIDDEN = 2816
PLE_DIM = 256
EPS = 1e-6
LOG2E = math.log2(math.e)
QKV_WIDTH = 3 * ATT_WIDTH
GATE_COL0 = 7 * BRANCH_WIDTH
REST_WIDTH = GATE_COL0 + 4 * D_MODEL

TM_PROJ = 512
TQ = 256
TK = 256
FAR_LOG2 = 2
TK_FAR = TK << FAR_LOG2
TM_MIX = 256
TM_FFN = 512
SUBLANES = 8
MXU_COLS = 256
VMEM_LIMIT = 56 * 1024 * 1024


def _t5_large_thresholds():
    max_exact = REL_BUCKETS // 2
    n = np.arange(1, 4 * REL_MAX_DIST, dtype=np.float64)
    large = max_exact + (np.log(n / max_exact) / math.log(REL_MAX_DIST / max_exact)
                         * (REL_BUCKETS - max_exact)).astype(np.int64)
    bucket = np.where(n < max_exact, n.astype(np.int64), np.minimum(large, REL_BUCKETS - 1))
    return [int(np.argmax(bucket >= b)) + 1 for b in range(max_exact + 1, REL_BUCKETS)]


T5_THRESHOLDS = _t5_large_thresholds()
assert T5_THRESHOLDS[-1] <= TK + 1


def _const_spec(block_shape, index_map):
    return pl.BlockSpec(block_shape, index_map, pipeline_mode=pl.Buffered(1))


def _rmsnorm(x, g):
    return x * lax.rsqrt(jnp.mean(x * x, axis=-1, keepdims=True) + EPS) * g


def _bf16(x):
    return x.astype(jnp.bfloat16)


def _dot(a, b):
    return jnp.dot(a, b, preferred_element_type=jnp.float32)


def _bias_kernel(table_ref, o_ref):
    h = pl.program_id(0)
    d = pl.program_id(1)
    i = lax.broadcasted_iota(jnp.int32, (TQ, TK), 0)
    j = lax.broadcasted_iota(jnp.int32, (TQ, TK), 1)
    n = jnp.maximum(d * TK + i - j, 0)
    max_exact = REL_BUCKETS // 2
    large = jnp.full((TQ, TK), max_exact, jnp.int32)
    for thr in T5_THRESHOLDS:
        large = large + (n >= thr).astype(jnp.int32)
    bucket = jnp.where(n < max_exact, n, large)
    val = jnp.zeros((TQ, TK), jnp.float32)
    for b in range(REL_BUCKETS):
        val = jnp.where(bucket == b, table_ref[b, h], val)
    o_ref[0, 0] = val * LOG2E


def _bias_blocks(rel_bias):
    return pl.pallas_call(
        _bias_kernel,
        grid=(ATT_HEADS, 2),
        in_specs=[pl.BlockSpec(memory_space=pltpu.SMEM)],
        out_specs=pl.BlockSpec((1, 1, TQ, TK), lambda h, d: (h, d, 0, 0)),
        out_shape=jax.ShapeDtypeStruct((ATT_HEADS, 2, TQ, TK), jnp.float32),
        name="t5_bias_blocks",
    )(rel_bias)


def _qkv_kernel(x_ref, g_ref, w_ref, o_ref):
    hb = _bf16(_rmsnorm(x_ref[...], g_ref[0]))
    z = _dot(hb, w_ref[0])
    o_ref[:, :ATT_WIDTH] = _bf16(z[:, :ATT_WIDTH] * (ATT_QK_DIM ** -0.5 * LOG2E))
    o_ref[:, ATT_WIDTH:] = _bf16(z[:, ATT_WIDTH:])


def _qkv_proj(x2d, g, w_qkv, layer):
    t = x2d.shape[0]
    return pl.pallas_call(
        _qkv_kernel,
        grid=(t // TM_PROJ,),
        in_specs=[
            pl.BlockSpec((TM_PROJ, D_MODEL), lambda i: (i, 0)),
            _const_spec((1, 1, D_MODEL), lambda i: (layer, 0, 0)),
            _const_spec((1, D_MODEL, QKV_WIDTH), lambda i: (layer, 0, 0)),
        ],
        out_specs=pl.BlockSpec((TM_PROJ, QKV_WIDTH), lambda i: (i, 0)),
        out_shape=jax.ShapeDtypeStruct((t, QKV_WIDTH), jnp.bfloat16),
        compiler_params=pltpu.CompilerParams(
            dimension_semantics=("arbitrary",), vmem_limit_bytes=VMEM_LIMIT),
        name="qkv_proj",
    )(x2d, g, w_qkv)


def _attn_kernel(lam_init, table_ref, q_ref, k_ref, v_ref, bias_ref, lam_ref, subg_ref, o_ref,
                 m_sc, acc_sc):
    i = pl.program_id(1)
    rows = 2 * TQ
    dv = ATT_V_DIM
    lane = lax.broadcasted_iota(jnp.int32, (TQ, dv), 1)

    qs = []
    for h in range(ATT_HEADS):
        q = q_ref[0, :, h * dv:(h + 1) * dv]
        zero = jnp.zeros_like(q)
        qs.append(jnp.concatenate([jnp.where(lane < ATT_QK_DIM, q, zero),
                                   jnp.where(lane >= ATT_QK_DIM, q, zero)], axis=0))

    def scores(h, start, width):
        k = k_ref[0, pl.ds(start, width), h * dv:(h + 1) * dv]
        return lax.dot_general(qs[h], k, (((1,), (1,)), ((), ())), preferred_element_type=jnp.float32)

    def update(h, s, start, width, add_bias, first):
        v = v_ref[0, pl.ds(start, width), h * dv:(h + 1) * dv]
        if add_bias is not None:
            s = add_bias(s)
        m_blk = jnp.max(s, axis=1, keepdims=True)
        m_new = jnp.broadcast_to(m_blk, (rows, dv)) if first else jnp.maximum(m_sc[h], m_blk)
        p = jnp.exp2(_bf16(s - jnp.concatenate([m_new] * (width // dv), axis=1)))
        ones = jnp.ones((width, dv), jnp.bfloat16)
        pv = _dot(p, jnp.concatenate([v, ones], axis=1))
        if first:
            acc_sc[h] = pv
            m_sc[h] = m_new - table_ref[REL_BUCKETS - 1, h] * LOG2E
        else:
            alpha = jnp.exp2(m_sc[h] - m_new)
            acc_sc[h] = jnp.concatenate([alpha, alpha], axis=1) * acc_sc[h] + pv
            m_sc[h] = m_new

    def blocks(start, width, bias_of_head, first=False):
        s_all = [scores(h, start, width) for h in range(ATT_HEADS)]
        for h in range(ATT_HEADS):
            update(h, s_all[h], start, width, bias_of_head(h) if bias_of_head else None, first)

    row = lax.broadcasted_iota(jnp.int32, (TQ, TK), 0)
    col = lax.broadcasted_iota(jnp.int32, (TQ, TK), 1)
    causal = (row >= col)[None]

    def diag_bias(h):
        def f(s):
            s3 = s.reshape(2, TQ, TK) + bias_ref[h, 0][None]
            return jnp.where(causal, s3, NEG_LOGIT).reshape(rows, TK)
        return f

    def tail_bias(h):
        def f(s):
            prev = (s[:, :TK].reshape(2, TQ, TK) + bias_ref[h, 1][None]).reshape(rows, TK)
            return jnp.concatenate([prev, diag_bias(h)(s[:, TK:])], axis=1)
        return f

    @pl.when(i == 0)
    def _():
        blocks(0, TK, diag_bias, first=True)

    @pl.when(i >= 1)
    def _():
        blocks(pl.multiple_of((i - 1) * TK, TK), 2 * TK, tail_bias, first=True)

    n_far = jnp.maximum(i - 1, 0)
    n_wide = lax.shift_right_logical(n_far, FAR_LOG2)

    def far_body(j, carry):
        blocks(pl.multiple_of(j * TK_FAR, TK_FAR), TK_FAR, None)
        return carry

    lax.fori_loop(0, n_wide, far_body, 0)

    start = n_wide * TK_FAR
    for bit in reversed(range(FAR_LOG2)):
        width = TK << bit

        @pl.when(((n_far >> bit) & 1) == 1)
        def _(start=start, width=width):
            blocks(pl.multiple_of(start, TK), width, None)

        start = start + ((n_far >> bit) & 1) * width

    lv = lam_ref[0]
    lam = (jnp.exp(jnp.sum(lv[0:1] * lv[1:2], axis=1, keepdims=True))
           - jnp.exp(jnp.sum(lv[2:3] * lv[3:4], axis=1, keepdims=True)) + lam_init)
    for h in range(ATT_HEADS):
        acc = acc_sc[h]
        o = acc[:, :dv] / acc[:, dv:]
        o = o[:TQ] - lam * o[TQ:]
        o = _rmsnorm(o, subg_ref[0]) * (1.0 - lam_init)
        o_ref[0, :, h * dv:(h + 1) * dv] = _bf16(o)


def _diff_attention(qkv, rel_bias, bias_blocks, att_lambda, att_subnorm_g, layer, lam_init):
    b, s, _ = qkv.shape
    nh = ATT_HEADS
    return pl.pallas_call(
        functools.partial(_attn_kernel, lam_init),
        grid=(b, s // TQ),
        in_specs=[
            pl.BlockSpec(memory_space=pltpu.SMEM),
            pl.BlockSpec((1, TQ, ATT_WIDTH), lambda bi, i: (bi, i, 0)),
            pl.BlockSpec((1, s, ATT_WIDTH), lambda bi, i: (bi, 0, 1)),
            pl.BlockSpec((1, s, ATT_WIDTH), lambda bi, i: (bi, 0, 2)),
            _const_spec((nh, 2, TQ, TK), lambda bi, i: (0, 0, 0, 0)),
            _const_spec((1, 4, ATT_QK_DIM), lambda bi, i: (layer, 0, 0)),
            _const_spec((1, 1, ATT_V_DIM), lambda bi, i: (layer, 0, 0)),
        ],
        out_specs=pl.BlockSpec((1, TQ, ATT_WIDTH), lambda bi, i: (bi, i, 0)),
        out_shape=jax.ShapeDtypeStruct((b, s, ATT_WIDTH), jnp.bfloat16),
        scratch_shapes=[
            pltpu.VMEM((nh, 2 * TQ, ATT_V_DIM), jnp.float32),
            pltpu.VMEM((nh, 2 * TQ, 2 * ATT_V_DIM), jnp.float32),
        ],
        compiler_params=pltpu.CompilerParams(
            dimension_semantics=("arbitrary", "arbitrary"), vmem_limit_bytes=VMEM_LIMIT),
        name="diff_attention",
    )(rel_bias, qkv, qkv, qkv, bias_blocks, att_lambda, att_subnorm_g)


def _strip_perm(tm):
    npos = tm // SUBLANES
    r = lax.broadcasted_iota(jnp.int32, (tm, tm), 0)
    c = lax.broadcasted_iota(jnp.int32, (tm, tm), 1)
    to_strip = jnp.where(c == (r & (SUBLANES - 1)) * npos + (r >> 3), 1.0, 0.0)
    to_time = jnp.where(r == (c & (SUBLANES - 1)) * npos + (c >> 3), 1.0, 0.0)
    return _bf16(to_strip), _bf16(to_time)


def _tiles(v):
    return [v[j * SUBLANES:(j + 1) * SUBLANES, :] for j in range(v.shape[0] // SUBLANES)]


def _strip_conv(x, prev_ref, w_ref, ktaps):
    tm = x.shape[0]
    npos = tm // SUBLANES
    prev = prev_ref[...]
    prev_ref[...] = x
    sub = lax.broadcasted_iota(jnp.int32, x.shape, 0) & (SUBLANES - 1)
    lo = npos - ktaps
    prev_next = jnp.concatenate([prev[SUBLANES:], prev[:SUBLANES]], axis=0)
    y = jnp.where(sub == SUBLANES - 1, prev_next, x)[lo * SUBLANES:]
    back = _tiles(pltpu.roll(y, 1, 0))
    cur = _tiles(x)
    out = []
    for pos in range(npos):
        acc = None
        for k in range(ktaps):
            shift = ktaps - 1 - k
            src = cur[pos - shift] if pos >= shift else back[pos - shift + npos - lo]
            term = w_ref[0, k:k + 1, :] * src
            acc = term if acc is None else acc + term
        out.append(acc)
    return jnp.concatenate(out, axis=0)


def _strip_scan(a, u, carry_ref):
    at, ut = _tiles(a), _tiles(u)
    h, p = [ut[0]], [at[0]]
    for j in range(1, len(at)):
        h.append(at[j] * h[-1] + ut[j])
        p.append(at[j] * p[-1])
    e, f = h[-1], p[-1]
    sub = lax.broadcasted_iota(jnp.int32, e.shape, 0)
    d = 1
    while d < SUBLANES:
        e_s = pltpu.roll(e, d, 0)
        f_s = pltpu.roll(f, d, 0)
        keep = sub >= d
        e = jnp.where(keep, f * e_s + e, e)
        f = jnp.where(keep, f * f_s, f)
        d *= 2
    c0 = carry_ref[0:1, :]
    ends = e + f * c0
    carry_in = jnp.where(sub == 0, c0, pltpu.roll(ends, 1, 0))
    carry_ref[0:1, :] = ends[SUBLANES - 1:SUBLANES, :]
    return jnp.concatenate([h[j] + p[j] * carry_in for j in range(len(at))], axis=0)


def _mix_kernel(x_ref, yatt_ref, gpre_ref, w_ref, wbd_ref, bbd_ref, lcw_ref, lcb_ref, llam_ref,
                scw_ref, cfw_ref, cfb_ref, cflg_ref, cflb_ref, gb_ref, wbr_ref, wo_ref, gpost_ref,
                o_ref, lprev, sprev, cprev, hcar):
    bw = BRANCH_WIDTH

    @pl.when(pl.program_id(1) == 0)
    def _():
        lprev[...] = jnp.zeros(lprev.shape, jnp.float32)
        sprev[...] = jnp.zeros(sprev.shape, jnp.float32)
        cprev[...] = jnp.zeros(cprev.shape, jnp.float32)
        hcar[...] = jnp.zeros(hcar.shape, jnp.float32)

    to_strip, to_time = _strip_perm(TM_MIX)
    x = x_ref[0]
    hb = _bf16(_dot(to_strip, _bf16(_rmsnorm(x, gpre_ref[0]))))
    yatt = _bf16(_dot(to_strip, yatt_ref[0]))

    def proj(idx):
        c0 = idx * bw
        return _dot(hb, w_ref[0, :, c0:c0 + bw])

    def gate(jb):
        c0 = GATE_COL0 + jb * D_MODEL
        return jnp.tanh(_dot(hb, w_ref[0, :, c0:c0 + D_MODEL]) + gb_ref[0, jb:jb + 1, :]) + 1.0

    u = proj(5) * jax.nn.sigmoid(proj(6))
    lru_x = proj(0)
    cx = proj(3) * proj(4)
    sc_b = proj(2)
    lru_g = jax.nn.gelu(proj(1))

    xc = _strip_conv(lru_x, lprev, lcw_ref, LRU_CONV) + lcb_ref[0]
    ri = jax.nn.sigmoid(_dot(_bf16(xc), wbd_ref[0]) + bbd_ref[0])
    gates = [gate(jb) for jb in range(4)]
    merged = gates[0] * _dot(yatt, wbr_ref[0, 0])

    r = ri[:, :bw]
    ig = ri[:, bw:]
    log_a = -LRU_C * r * jax.nn.softplus(-llam_ref[0])
    a = jnp.exp(log_a)
    sq = -jnp.tanh(log_a) * (a * a + 1.0)
    mult = jnp.where(sq > 0.0, sq * lax.rsqrt(sq), 0.0)
    y_lru = _strip_scan(a, mult * (ig * xc), hcar) * lru_g

    y_sc = sc_b * _strip_conv(cx, sprev, scw_ref, SC_CONV)

    u = _strip_conv(u, cprev, cfw_ref, CF_CONV) + cfb_ref[0]
    mu = jnp.mean(u, axis=-1, keepdims=True)
    uc = u - mu
    var = jnp.mean(uc * uc, axis=-1, keepdims=True)
    ln = uc * lax.rsqrt(var + EPS) * cflg_ref[0] + cflb_ref[0]
    y_cf = ln * jax.nn.sigmoid(ln)

    for jb, y in ((1, y_lru), (2, y_sc), (3, y_cf)):
        merged = merged + gates[jb] * _dot(_bf16(y), wbr_ref[0, jb])

    merged = _bf16(_dot(to_time, _bf16(merged)))
    o = _dot(merged, wo_ref[0])
    o_ref[0] = x + _rmsnorm(o, gpost_ref[0])


def _mixers(x, yatt, layer, gpre, w_rest, wbd, bbd, lcw, lcb, llam, scw, cfw, cfb, cflg, cflb,
            gate_b, w_branch, w_o, gpost):
    b, s, d = x.shape
    bw = BRANCH_WIDTH

    def lspec(*shape):
        nd = len(shape)
        return _const_spec((1,) + shape, lambda bi, i: (layer,) + (0,) * nd)

    return pl.pallas_call(
        _mix_kernel,
        grid=(b, s // TM_MIX),
        in_specs=[
            pl.BlockSpec((1, TM_MIX, d), lambda bi, i: (bi, i, 0)),
            pl.BlockSpec((1, TM_MIX, ATT_WIDTH), lambda bi, i: (bi, i, 0)),
            lspec(1, d),
            lspec(d, REST_WIDTH),
            lspec(bw, 2 * bw),
            lspec(1, 2 * bw),
            lspec(LRU_CONV, bw),
            lspec(1, bw),
            lspec(1, bw),
            lspec(SC_CONV, bw),
            lspec(CF_CONV, bw),
            lspec(1, bw),
            lspec(1, bw),
            lspec(1, bw),
            lspec(4, d),
            lspec(4, bw, d),
            lspec(d, d),
            lspec(1, d),
        ],
        out_specs=pl.BlockSpec((1, TM_MIX, d), lambda bi, i: (bi, i, 0)),
        out_shape=jax.ShapeDtypeStruct((b, s, d), jnp.float32),
        scratch_shapes=[
            pltpu.VMEM((TM_MIX, bw), jnp.float32),
            pltpu.VMEM((TM_MIX, bw), jnp.float32),
            pltpu.VMEM((TM_MIX, bw), jnp.float32),
            pltpu.VMEM((SUBLANES, bw), jnp.float32),
        ],
        compiler_params=pltpu.CompilerParams(
            dimension_semantics=("arbitrary", "arbitrary"), vmem_limit_bytes=VMEM_LIMIT),
        name="mixers_merge",
    )(x, yatt, gpre, w_rest, wbd, bbd, lcw, lcb, llam, scw, cfw, cfb, cflg, cflb, gate_b,
      w_branch, w_o, gpost)


def _ffn_kernel(x_ref, p_ref, gpre_ref, win_ref, wout_ref, gpost_ref, wple_ref, gple_ref,
                wgate_ref, o_ref):
    x = x_ref[...]
    hb = _bf16(_rmsnorm(x, gpre_ref[0]))
    gt = _dot(hb, win_ref[0, :, :FFN_HIDDEN])
    up = _dot(hb, win_ref[0, :, FFN_HIDDEN:])
    act = _bf16(gt * jax.nn.sigmoid(gt) * up)
    x = x + _rmsnorm(_dot(act, wout_ref[0]), gpost_ref[0])
    e = _dot(_bf16(p_ref[0]), wple_ref[0])
    ge = jax.nn.sigmoid(_dot(_bf16(_rmsnorm(x, gple_ref[0])), wgate_ref[0]))
    o_ref[...] = x + ge * e


def _ffn_ple(x2d, p3d, layer, gpre, w_ffn_in, w_out, gpost, w_ple, g_ple, w_ple_gate):
    t, d = x2d.shape

    def lspec(*shape):
        nd = len(shape)
        return _const_spec((1,) + shape, lambda i: (layer,) + (0,) * nd)

    return pl.pallas_call(
        _ffn_kernel,
        grid=(t // TM_FFN,),
        in_specs=[
            pl.BlockSpec((TM_FFN, d), lambda i: (i, 0)),
            pl.BlockSpec((1, TM_FFN, PLE_DIM), lambda i: (layer, i, 0)),
            lspec(1, d),
            lspec(d, 2 * FFN_HIDDEN),
            lspec(FFN_HIDDEN, d),
            lspec(1, d),
            lspec(PLE_DIM, d),
            lspec(1, d),
            lspec(d, d),
        ],
        out_specs=pl.BlockSpec((TM_FFN, d), lambda i: (i, 0)),
        out_shape=jax.ShapeDtypeStruct((t, d), jnp.float32),
        compiler_params=pltpu.CompilerParams(
            dimension_semantics=("arbitrary",), vmem_limit_bytes=VMEM_LIMIT),
        name="ffn_ple",
    )(x2d, p3d, gpre, w_ffn_in, w_out, gpost, w_ple, g_ple, w_ple_gate)


def _block_diag(w):
    l, h, i, j = w.shape
    eye = jnp.eye(h, dtype=w.dtype)
    return (w[:, :, :, None, :] * eye[None, :, None, :, None]).reshape(l, h * i, h * j)


def kernel(x, p, rel_bias, g_pre_mix, w_in, att_lambda, att_subnorm_g, lru_conv_w, lru_conv_b, lru_wa, lru_ba, lru_wx, lru_bx, lru_lambda, sc_conv_w, cf_conv_w, cf_conv_b, cf_ln_g, cf_ln_b, gate_b, w_branch, w_o, g_post_mix, g_pre_ffn, w_ffn_in, w_ffn_out, g_post_ffn, w_ple_in, g_ple, w_ple_gate):
    b, s, d = x.shape
    depth = w_in.shape[0]
    t = b * s
    bf = jnp.bfloat16

    def row(v):
        return v[:, None, :]

    w_qkv = w_in[:, :, :QKV_WIDTH].astype(bf)
    col_scale = jnp.where(jnp.arange(REST_WIDTH) >= GATE_COL0, 0.5, 1.0).astype(jnp.float32)
    w_rest = (w_in[:, :, QKV_WIDTH:] * col_scale).astype(bf)
    gate_b_half = gate_b * 0.5
    wbd = jnp.concatenate([_block_diag(lru_wa), _block_diag(lru_wx)], axis=-1).astype(bf)
    bbd = jnp.concatenate([lru_ba.reshape(depth, 1, LRU_WIDTH), lru_bx.reshape(depth, 1, LRU_WIDTH)], axis=-1)
    w_branch_half = (w_branch * 0.5).astype(bf)
    w_o_b = w_o.astype(bf)
    w_ffn_in_b = w_ffn_in.astype(bf)
    w_out_b = w_ffn_out.astype(bf)
    w_ple_b = w_ple_in.astype(bf)
    w_ple_gate_b = w_ple_gate.astype(bf)
    p3d = p.reshape(depth, t, PLE_DIM)

    bias_blocks = _bias_blocks(rel_bias)

    for layer in range(depth):
        lam_init = 0.8 - 0.6 * math.exp(-0.3 * layer)
        qkv = _qkv_proj(x.reshape(t, d), row(g_pre_mix), w_qkv, layer)
        yatt = _diff_attention(qkv.reshape(b, s, QKV_WIDTH), rel_bias, bias_blocks, att_lambda,
                               row(att_subnorm_g), layer, lam_init)
        x = _mixers(x, yatt, layer, row(g_pre_mix), w_rest, wbd, bbd, lru_conv_w, row(lru_conv_b),
                    row(lru_lambda), sc_conv_w, cf_conv_w, row(cf_conv_b), row(cf_ln_g), row(cf_ln_b),
                    gate_b_half, w_branch_half, w_o_b, row(g_post_mix))
        x = _ffn_ple(x.reshape(t, d), p3d, layer, row(g_pre_ffn), w_ffn_in_b, w_out_b,
                     row(g_post_ffn), w_ple_b, row(g_ple), w_ple_gate_b).reshape(b, s, d)
    return x
```

```python
import functools
import math

import numpy as np
import jax
import jax.numpy as jnp
from jax import lax
from jax.experimental import pallas as pl
from jax.experimental.pallas import tpu as pltpu

D_MODEL = 1024
ATT_HEADS = 4
ATT_QK_DIM = 64
ATT_V_DIM = 128
ATT_WIDTH = 512
REL_BUCKETS = 32
REL_MAX_DIST = 128
NEG_LOGIT = -1e30
LRU_WIDTH = 512
LRU_BLOCKS = 8
LRU_CONV = 4
LRU_C = 8.0
SC_CONV = 3
CF_CONV = 31
BRANCH_WIDTH = 512
FFN_HIDDEN = 2816
PLE_DIM = 256
EPS = 1e-6
LOG2E = math.log2(math.e)
QKV_WIDTH = 3 * ATT_WIDTH
GATE_COL0 = 7 * BRANCH_WIDTH
REST_WIDTH = GATE_COL0 + 4 * D_MODEL

TM_PROJ = 512
TQ = 256
TK = 256
FAR_LOG2 = 2
TK_FAR = TK << FAR_LOG2
TM_MIX = 512
STRIP_ROWS = 256
TM_FFN = 512
SUBLANES = 8
MXU_COLS = 256
MXU_COLS = 256
VMEM_LIMIT = 56 * 1024 * 1024


def _t5_large_thresholds():
    max_exact = REL_BUCKETS // 2
    n = np.arange(1, 4 * REL_MAX_DIST, dtype=np.float64)
    large = max_exact + (np.log(n / max_exact) / math.log(REL_MAX_DIST / max_exact)
                         * (REL_BUCKETS - max_exact)).astype(np.int64)
    bucket = np.where(n < max_exact, n.astype(np.int64), np.minimum(large, REL_BUCKETS - 1))
    return [int(np.argmax(bucket >= b)) + 1 for b in range(max_exact + 1, REL_BUCKETS)]


T5_THRESHOLDS = _t5_large_thresholds()
assert T5_THRESHOLDS[-1] <= TK + 1


def _const_spec(block_shape, index_map):
    return pl.BlockSpec(block_shape, index_map, pipeline_mode=pl.Buffered(1))


def _rmsnorm(x, g):
    return x * lax.rsqrt(jnp.mean(x * x, axis=-1, keepdims=True) + EPS) * g


def _bf16(x):
    return x.astype(jnp.bfloat16)


def _dot(a, b):
    return jnp.dot(a, b, preferred_element_type=jnp.float32)


def _bias_kernel(table_ref, o_ref):
    h = pl.program_id(0)
    d = pl.program_id(1)
    i = lax.broadcasted_iota(jnp.int32, (TQ, TK), 0)
    j = lax.broadcasted_iota(jnp.int32, (TQ, TK), 1)
    n = jnp.maximum(d * TK + i - j, 0)
    max_exact = REL_BUCKETS // 2
    large = jnp.full((TQ, TK), max_exact, jnp.int32)
    for thr in T5_THRESHOLDS:
        large = large + (n >= thr).astype(jnp.int32)
    bucket = jnp.where(n < max_exact, n, large)
    val = jnp.zeros((TQ, TK), jnp.float32)
    for b in range(REL_BUCKETS):
        val = jnp.where(bucket == b, table_ref[b, h], val)
    o_ref[0, 0] = val * LOG2E


def _bias_blocks(rel_bias):
    return pl.pallas_call(
        _bias_kernel,
        grid=(ATT_HEADS, 2),
        in_specs=[pl.BlockSpec(memory_space=pltpu.SMEM)],
        out_specs=pl.BlockSpec((1, 1, TQ, TK), lambda h, d: (h, d, 0, 0)),
        out_shape=jax.ShapeDtypeStruct((ATT_HEADS, 2, TQ, TK), jnp.float32),
        name="t5_bias_blocks",
    )(rel_bias)


def _qkv_kernel(x_ref, g_ref, w_ref, o_ref):
    hb = _bf16(_rmsnorm(x_ref[...], g_ref[0]))
    z = _dot(hb, w_ref[0])
    o_ref[:, :ATT_WIDTH] = _bf16(z[:, :ATT_WIDTH] * (ATT_QK_DIM ** -0.5 * LOG2E))
    o_ref[:, ATT_WIDTH:] = _bf16(z[:, ATT_WIDTH:])


def _qkv_proj(x2d, g, w_qkv, layer):
    t = x2d.shape[0]
    return pl.pallas_call(
        _qkv_kernel,
        grid=(t // TM_PROJ,),
        in_specs=[
            pl.BlockSpec((TM_PROJ, D_MODEL), lambda i: (i, 0)),
            _const_spec((1, 1, D_MODEL), lambda i: (layer, 0, 0)),
            _const_spec((1, D_MODEL, QKV_WIDTH), lambda i: (layer, 0, 0)),
        ],
        out_specs=pl.BlockSpec((TM_PROJ, QKV_WIDTH), lambda i: (i, 0)),
        out_shape=jax.ShapeDtypeStruct((t, QKV_WIDTH), jnp.bfloat16),
        compiler_params=pltpu.CompilerParams(
            dimension_semantics=("arbitrary",), vmem_limit_bytes=VMEM_LIMIT),
        name="qkv_proj",
    )(x2d, g, w_qkv)


def _attn_kernel(lam_init, table_ref, q_ref, k_ref, v_ref, bias_ref, lam_ref, subg_ref, o_ref,
                 m_sc, acc_sc):
    i = pl.program_id(1)
    rows = 2 * TQ
    dv = ATT_V_DIM
    lane = lax.broadcasted_iota(jnp.int32, (TQ, dv), 1)

    qs = []
    for h in range(ATT_HEADS):
        q = q_ref[0, :, h * dv:(h + 1) * dv]
        zero = jnp.zeros_like(q)
        qs.append(jnp.concatenate([jnp.where(lane < ATT_QK_DIM, q, zero),
                                   jnp.where(lane >= ATT_QK_DIM, q, zero)], axis=0))

    def scores(h, start, width):
        k = k_ref[0, pl.ds(start, width), h * dv:(h + 1) * dv]
        return lax.dot_general(qs[h], k, (((1,), (1,)), ((), ())), preferred_element_type=jnp.float32)

    def update(h, s, start, width, add_bias, first):
        v = v_ref[0, pl.ds(start, width), h * dv:(h + 1) * dv]
        if add_bias is not None:
            s = add_bias(s)
        m_blk = jnp.max(s, axis=1, keepdims=True)
        m_new = jnp.broadcast_to(m_blk, (rows, dv)) if first else jnp.maximum(m_sc[h], m_blk)
        p = jnp.exp2(_bf16(s - jnp.concatenate([m_new] * (width // dv), axis=1)))
        ones = jnp.ones((width, dv), jnp.bfloat16)
        pv = _dot(p, jnp.concatenate([v, ones], axis=1))
        if first:
            acc_sc[h] = pv
            m_sc[h] = m_new - table_ref[REL_BUCKETS - 1, h] * LOG2E
        else:
            alpha = jnp.exp2(m_sc[h] - m_new)
            acc_sc[h] = jnp.concatenate([alpha, alpha], axis=1) * acc_sc[h] + pv
            m_sc[h] = m_new

    def blocks(start, width, bias_of_head, first=False):
        s_all = [scores(h, start, width) for h in range(ATT_HEADS)]
        for h in range(ATT_HEADS):
            update(h, s_all[h], start, width, bias_of_head(h) if bias_of_head else None, first)

    row = lax.broadcasted_iota(jnp.int32, (TQ, TK), 0)
    col = lax.broadcasted_iota(jnp.int32, (TQ, TK), 1)
    causal = (row >= col)[None]

    def diag_bias(h):
        def f(s):
            s3 = s.reshape(2, TQ, TK) + bias_ref[h, 0][None]
            return jnp.where(causal, s3, NEG_LOGIT).reshape(rows, TK)
        return f

    def tail_bias(h):
        def f(s):
            prev = (s[:, :TK].reshape(2, TQ, TK) + bias_ref[h, 1][None]).reshape(rows, TK)
            return jnp.concatenate([prev, diag_bias(h)(s[:, TK:])], axis=1)
        return f

    @pl.when(i == 0)
    def _():
        blocks(0, TK, diag_bias, first=True)

    @pl.when(i >= 1)
    def _():
        blocks(pl.multiple_of((i - 1) * TK, TK), 2 * TK, tail_bias, first=True)

    n_far = jnp.maximum(i - 1, 0)
    n_wide = lax.shift_right_logical(n_far, FAR_LOG2)

    def far_body(j, carry):
        blocks(pl.multiple_of(j * TK_FAR, TK_FAR), TK_FAR, None)
        return carry

    lax.fori_loop(0, n_wide, far_body, 0)

    start = n_wide * TK_FAR
    for bit in reversed(range(FAR_LOG2)):
        width = TK << bit

        @pl.when(((n_far >> bit) & 1) == 1)
        def _(start=start, width=width):
            blocks(pl.multiple_of(start, TK), width, None)

        start = start + ((n_far >> bit) & 1) * width

    lv = lam_ref[0]
    lam = (jnp.exp(jnp.sum(lv[0:1] * lv[1:2], axis=1, keepdims=True))
           - jnp.exp(jnp.sum(lv[2:3] * lv[3:4], axis=1, keepdims=True)) + lam_init)
    for h in range(ATT_HEADS):
        acc = acc_sc[h]
        o = acc[:, :dv] / acc[:, dv:]
        o = o[:TQ] - lam * o[TQ:]
        o = _rmsnorm(o, subg_ref[0]) * (1.0 - lam_init)
        o_ref[0, :, h * dv:(h + 1) * dv] = _bf16(o)


def _diff_attention(qkv, rel_bias, bias_blocks, att_lambda, att_subnorm_g, layer, lam_init):
    b, s, _ = qkv.shape
    nh = ATT_HEADS
    return pl.pallas_call(
        functools.partial(_attn_kernel, lam_init),
        grid=(b, s // TQ),
        in_specs=[
            pl.BlockSpec(memory_space=pltpu.SMEM),
            pl.BlockSpec((1, TQ, ATT_WIDTH), lambda bi, i: (bi, i, 0)),
            pl.BlockSpec((1, s, ATT_WIDTH), lambda bi, i: (bi, 0, 1)),
            pl.BlockSpec((1, s, ATT_WIDTH), lambda bi, i: (bi, 0, 2)),
            _const_spec((nh, 2, TQ, TK), lambda bi, i: (0, 0, 0, 0)),
            _const_spec((1, 4, ATT_QK_DIM), lambda bi, i: (layer, 0, 0)),
            _const_spec((1, 1, ATT_V_DIM), lambda bi, i: (layer, 0, 0)),
        ],
        out_specs=pl.BlockSpec((1, TQ, ATT_WIDTH), lambda bi, i: (bi, i, 0)),
        out_shape=jax.ShapeDtypeStruct((b, s, ATT_WIDTH), jnp.bfloat16),
        scratch_shapes=[
            pltpu.VMEM((nh, 2 * TQ, ATT_V_DIM), jnp.float32),
            pltpu.VMEM((nh, 2 * TQ, 2 * ATT_V_DIM), jnp.float32),
        ],
        compiler_params=pltpu.CompilerParams(
            dimension_semantics=("arbitrary", "arbitrary"), vmem_limit_bytes=VMEM_LIMIT),
        name="diff_attention",
    )(rel_bias, qkv, qkv, qkv, bias_blocks, att_lambda, att_subnorm_g)


def _strip_perm():
    npos = STRIP_ROWS // SUBLANES
    r = np.arange(STRIP_ROWS)
    time_of_row = (r % SUBLANES) * npos + r // SUBLANES
    to_strip = np.zeros((STRIP_ROWS, STRIP_ROWS), np.float32)
    to_strip[r, time_of_row] = 1.0
    return np.stack([to_strip, to_strip.T])


def _groups(v):
    return [v[g:g + STRIP_ROWS] for g in range(0, v.shape[0], STRIP_ROWS)]


def _permute_groups(perm, v):
    return _bf16(jnp.concatenate([_dot(perm, g) for g in _groups(v)], axis=0))


def _tiles(v):
    return [v[j * SUBLANES:(j + 1) * SUBLANES, :] for j in range(v.shape[0] // SUBLANES)]


def _strip_conv_group(x, prev, w_ref, ktaps):
    npos = STRIP_ROWS // SUBLANES
    sub = lax.broadcasted_iota(jnp.int32, x.shape, 0) & (SUBLANES - 1)
    lo = npos - ktaps
    prev_next = jnp.concatenate([prev[SUBLANES:], prev[:SUBLANES]], axis=0)
    y = jnp.where(sub == SUBLANES - 1, prev_next, x)[lo * SUBLANES:]
    back = _tiles(pltpu.roll(y, 1, 0))
    cur = _tiles(x)
    out = []
    for pos in range(npos):
        acc = None
        for k in range(ktaps):
            shift = ktaps - 1 - k
            src = cur[pos - shift] if pos >= shift else back[pos - shift + npos - lo]
            term = w_ref[0, k:k + 1, :] * src
            acc = term if acc is None else acc + term
        out.append(acc)
    return jnp.concatenate(out, axis=0)


def _strip_conv(x, prev_ref, w_ref, ktaps):
    groups = _groups(x)
    before = [prev_ref[...]] + groups[:-1]
    prev_ref[...] = groups[-1]
    return jnp.concatenate([_strip_conv_group(g, p, w_ref, ktaps) for g, p in zip(groups, before)], axis=0)


def _strip_scan_group(a, u, c0):
    at, ut = _tiles(a), _tiles(u)
    h, p = [ut[0]], [at[0]]
    for j in range(1, len(at)):
        h.append(at[j] * h[-1] + ut[j])
        p.append(at[j] * p[-1])
    e, f = h[-1], p[-1]
    sub = lax.broadcasted_iota(jnp.int32, e.shape, 0)
    d = 1
    while d < SUBLANES:
        e_s = pltpu.roll(e, d, 0)
        f_s = pltpu.roll(f, d, 0)
        keep = sub >= d
        e = jnp.where(keep, f * e_s + e, e)
        f = jnp.where(keep, f * f_s, f)
        d *= 2
    ends = e + f * c0
    carry_in = jnp.where(sub == 0, c0, pltpu.roll(ends, 1, 0))
    h = jnp.concatenate([h[j] + p[j] * carry_in for j in range(len(at))], axis=0)
    return h, ends[SUBLANES - 1:SUBLANES, :]


def _strip_scan(a, u, carry_ref):
    carry = carry_ref[0:1, :]
    out = []
    for ag, ug in zip(_groups(a), _groups(u)):
        h, carry = _strip_scan_group(ag, ug, carry)
        out.append(h)
    carry_ref[0:1, :] = carry
    return jnp.concatenate(out, axis=0)


def _mix_kernel(x_ref, yatt_ref, perm_ref, gpre_ref, w_ref, wbd_ref, bbd_ref, lcw_ref, lcb_ref, llam_ref,
                scw_ref, cfw_ref, cfb_ref, cflg_ref, cflb_ref, gb_ref, wbr_ref, wo_ref, gpost_ref,
                o_ref, lprev, sprev, cprev, hcar):
    bw = BRANCH_WIDTH

    @pl.when(pl.program_id(1) == 0)
    def _():
        lprev[...] = jnp.zeros(lprev.shape, jnp.float32)
        sprev[...] = jnp.zeros(sprev.shape, jnp.float32)
        cprev[...] = jnp.zeros(cprev.shape, jnp.float32)
        hcar[...] = jnp.zeros(hcar.shape, jnp.float32)

    to_strip, to_time = perm_ref[0], perm_ref[1]
    x = x_ref[0]
    hb = _permute_groups(to_strip, _bf16(_rmsnorm(x, gpre_ref[0])))
    yatt = _permute_groups(to_strip, yatt_ref[0])

    def proj(idx):
        c0 = idx * bw
        return _dot(hb, w_ref[0, :, c0:c0 + bw])

    def gate(jb):
        c0 = GATE_COL0 + jb * D_MODEL
        return jnp.tanh(_dot(hb, w_ref[0, :, c0:c0 + D_MODEL]) + gb_ref[0, jb:jb + 1, :]) + 1.0

    u = proj(5) * jax.nn.sigmoid(proj(6))
    lru_x = proj(0)
    cx = proj(3) * proj(4)
    sc_b = proj(2)
    lru_g = jax.nn.gelu(proj(1))

    xc = _strip_conv(lru_x, lprev, lcw_ref, LRU_CONV) + lcb_ref[0]
    xcb = _bf16(xc)
    halves = (xcb[:, :MXU_COLS], xcb[:, MXU_COLS:])
    ri = jnp.concatenate([_dot(halves[n % 2], wbd_ref[0, n]) for n in range(4)], axis=1)
    ri = jax.nn.sigmoid(ri + bbd_ref[0])
    gates = [gate(jb) for jb in range(4)]
    merged = gates[0] * _dot(yatt, wbr_ref[0, 0])

    r = ri[:, :bw]
    ig = ri[:, bw:]
    log_a = -LRU_C * r * jax.nn.softplus(-llam_ref[0])
    a = jnp.exp(log_a)
    sq = -jnp.tanh(log_a) * (a * a + 1.0)
    mult = jnp.where(sq > 0.0, sq * lax.rsqrt(sq), 0.0)
    y_lru = _strip_scan(a, mult * (ig * xc), hcar) * lru_g

    y_sc = sc_b * _strip_conv(cx, sprev, scw_ref, SC_CONV)

    u = _strip_conv(u, cprev, cfw_ref, CF_CONV) + cfb_ref[0]
    mu = jnp.mean(u, axis=-1, keepdims=True)
    uc = u - mu
    var = jnp.mean(uc * uc, axis=-1, keepdims=True)
    ln = uc * lax.rsqrt(var + EPS) * cflg_ref[0] + cflb_ref[0]
    y_cf = ln * jax.nn.sigmoid(ln)

    for jb, y in ((1, y_lru), (2, y_sc), (3, y_cf)):
        merged = merged + gates[jb] * _dot(_bf16(y), wbr_ref[0, jb])

    merged = _permute_groups(to_time, _bf16(merged))
    o = _dot(merged, wo_ref[0])
    o_ref[0] = x + _rmsnorm(o, gpost_ref[0])


def _mixers(x, yatt, layer, gpre, w_rest, wbd, bbd, lcw, lcb, llam, scw, cfw, cfb, cflg, cflb,
            gate_b, w_branch, w_o, gpost):
    b, s, d = x.shape
    bw = BRANCH_WIDTH

    def lspec(*shape):
        nd = len(shape)
        return _const_spec((1,) + shape, lambda bi, i: (layer,) + (0,) * nd)

    return pl.pallas_call(
        _mix_kernel,
        grid=(b, s // TM_MIX),
        in_specs=[
            pl.BlockSpec((1, TM_MIX, d), lambda bi, i: (bi, i, 0)),
            pl.BlockSpec((1, TM_MIX, ATT_WIDTH), lambda bi, i: (bi, i, 0)),
            _const_spec((2, STRIP_ROWS, STRIP_ROWS), lambda bi, i: (0, 0, 0)),
            lspec(1, d),
            lspec(d, REST_WIDTH),
            lspec(4, MXU_COLS, MXU_COLS),
            lspec(1, 2 * bw),
            lspec(LRU_CONV, bw),
            lspec(1, bw),
            lspec(1, bw),
            lspec(SC_CONV, bw),
            lspec(CF_CONV, bw),
            lspec(1, bw),
            lspec(1, bw),
            lspec(1, bw),
            lspec(4, d),
            lspec(4, bw, d),
            lspec(d, d),
            lspec(1, d),
        ],
        out_specs=pl.BlockSpec((1, TM_MIX, d), lambda bi, i: (bi, i, 0)),
        out_shape=jax.ShapeDtypeStruct((b, s, d), jnp.float32),
        scratch_shapes=[
            pltpu.VMEM((STRIP_ROWS, bw), jnp.float32),
            pltpu.VMEM((STRIP_ROWS, bw), jnp.float32),
            pltpu.VMEM((STRIP_ROWS, bw), jnp.float32),
            pltpu.VMEM((SUBLANES, bw), jnp.float32),
        ],
        compiler_params=pltpu.CompilerParams(
            dimension_semantics=("arbitrary", "arbitrary"), vmem_limit_bytes=VMEM_LIMIT),
        name="mixers_merge",
    )(x, yatt, jnp.asarray(_strip_perm(), jnp.bfloat16), gpre, w_rest, wbd, bbd, lcw, lcb, llam, scw, cfw, cfb, cflg, cflb, gate_b,
      w_branch, w_o, gpost)


def _ffn_kernel(x_ref, p_ref, gpre_ref, win_ref, wout_ref, gpost_ref, wple_ref, gple_ref,
                wgate_ref, o_ref):
    x = x_ref[...]
    hb = _bf16(_rmsnorm(x, gpre_ref[0]))
    gt = _dot(hb, win_ref[0, :, :FFN_HIDDEN])
    up = _dot(hb, win_ref[0, :, FFN_HIDDEN:])
    act = _bf16(gt * jax.nn.sigmoid(gt) * up)
    x = x + _rmsnorm(_dot(act, wout_ref[0]), gpost_ref[0])
    e = _dot(_bf16(p_ref[0]), wple_ref[0])
    ge = jax.nn.sigmoid(_dot(_bf16(_rmsnorm(x, gple_ref[0])), wgate_ref[0]))
    o_ref[...] = x + ge * e


def _ffn_ple(x2d, p3d, layer, gpre, w_ffn_in, w_out, gpost, w_ple, g_ple, w_ple_gate):
    t, d = x2d.shape

    def lspec(*shape):
        nd = len(shape)
        return _const_spec((1,) + shape, lambda i: (layer,) + (0,) * nd)

    return pl.pallas_call(
        _ffn_kernel,
        grid=(t // TM_FFN,),
        in_specs=[
            pl.BlockSpec((TM_FFN, d), lambda i: (i, 0)),
            pl.BlockSpec((1, TM_FFN, PLE_DIM), lambda i: (layer, i, 0)),
            lspec(1, d),
            lspec(d, 2 * FFN_HIDDEN),
            lspec(FFN_HIDDEN, d),
            lspec(1, d),
            lspec(PLE_DIM, d),
            lspec(1, d),
            lspec(d, d),
        ],
        out_specs=pl.BlockSpec((TM_FFN, d), lambda i: (i, 0)),
        out_shape=jax.ShapeDtypeStruct((t, d), jnp.float32),
        compiler_params=pltpu.CompilerParams(
            dimension_semantics=("arbitrary",), vmem_limit_bytes=VMEM_LIMIT),
        name="ffn_ple",
    )(x2d, p3d, gpre, w_ffn_in, w_out, gpost, w_ple, g_ple, w_ple_gate)


def _block_diag(w):
    l, h, i, j = w.shape
    eye = jnp.eye(h, dtype=w.dtype)
    return (w[:, :, :, None, :] * eye[None, :, None, :, None]).reshape(l, h * i, h * j)


def kernel(x, p, rel_bias, g_pre_mix, w_in, att_lambda, att_subnorm_g, lru_conv_w, lru_conv_b, lru_wa, lru_ba, lru_wx, lru_bx, lru_lambda, sc_conv_w, cf_conv_w, cf_conv_b, cf_ln_g, cf_ln_b, gate_b, w_branch, w_o, g_post_mix, g_pre_ffn, w_ffn_in, w_ffn_out, g_post_ffn, w_ple_in, g_ple, w_ple_gate):
    b, s, d = x.shape
    depth = w_in.shape[0]
    t = b * s
    bf = jnp.bfloat16

    def row(v):
        return v[:, None, :]

    w_qkv = w_in[:, :, :QKV_WIDTH].astype(bf)
    col_scale = jnp.where(jnp.arange(REST_WIDTH) >= GATE_COL0, 0.5, 1.0).astype(jnp.float32)
    w_rest = (w_in[:, :, QKV_WIDTH:] * col_scale).astype(bf)
    gate_b_half = gate_b * 0.5
    per = MXU_COLS // (LRU_WIDTH // LRU_BLOCKS)
    wbd = jnp.stack([_block_diag(w[:, n * per:(n + 1) * per]) for w in (lru_wa, lru_wx)
                     for n in range(LRU_BLOCKS // per)], axis=1).astype(bf)
    bbd = jnp.concatenate([lru_ba.reshape(depth, 1, LRU_WIDTH), lru_bx.reshape(depth, 1, LRU_WIDTH)], axis=-1)
    w_branch_half = (w_branch * 0.5).astype(bf)
    w_o_b = w_o.astype(bf)
    w_ffn_in_b = w_ffn_in.astype(bf)
    w_out_b = w_ffn_out.astype(bf)
    w_ple_b = w_ple_in.astype(bf)
    w_ple_gate_b = w_ple_gate.astype(bf)
    p3d = p.reshape(depth, t, PLE_DIM)

    bias_blocks = _bias_blocks(rel_bias)

    for layer in range(depth):
        lam_init = 0.8 - 0.6 * math.exp(-0.3 * layer)
        qkv = _qkv_proj(x.reshape(t, d), row(g_pre_mix), w_qkv, layer)
        yatt = _diff_attention(qkv.reshape(b, s, QKV_WIDTH), rel_bias, bias_blocks, att_lambda,
                               row(att_subnorm_g), layer, lam_init)
        x = _mixers(x, yatt, layer, row(g_pre_mix), w_rest, wbd, bbd, lru_conv_w, row(lru_conv_b),
                    row(lru_lambda), sc_conv_w, cf_conv_w, row(cf_conv_b), row(cf_ln_g), row(cf_ln_b),
                    gate_b_half, w_branch_half, w_o_b, row(g_post_mix))
        x = _ffn_ple(x.reshape(t, d), p3d, layer, row(g_pre_ffn), w_ffn_in_b, w_out_b,
                     row(g_post_ffn), w_ple_b, row(g_ple), w_ple_gate_b).reshape(b, s, d)
    return x
```
